```python
import math
import jax, jax.numpy as jnp
from jax import lax
import numpy as np

D_MODEL = 2048
BATCH = 2
SEQ = 16384
DEPTH = 2

D_SSM = 1024
SSM_GROUP = 16
N_SSM_GROUPS = D_SSM // SSM_GROUP
SSM_STATE = 64
SSM_DT_MIN = 1e-3
SSM_DT_MAX = 1e-1
GLA_HEADS = 4
GLA_DK = 128
GLA_DV = 256
GLA_GATE_RANK = 16
GLA_GATE_NORM = 16.0
GLA_CHUNK = 32
DIFF_HEADS = 4
DIFF_DK = 128
DIFF_DV = 256
Q_BLOCK = 128
N_BRANCH = 3
D_FF = 5504
CONV_W = 3
EPS = 1e-6
IN_SPLITS = (D_SSM, GLA_HEADS * GLA_DK, GLA_HEADS * GLA_DK, GLA_HEADS * GLA_DV, GLA_HEADS * GLA_DV, GLA_GATE_RANK, DIFF_HEADS * 2 * DIFF_DK, DIFF_HEADS * 2 * DIFF_DK, DIFF_HEADS * DIFF_DV, N_BRANCH * D_MODEL)
D_IN = sum(IN_SPLITS)
ALIBI_SLOPES = tuple(2.0 ** (-8.0 * (h + 1) / DIFF_HEADS) for h in range(DIFF_HEADS))

kernel_name = 'hybrid_s5_gla_diffattn_convffn'


def _rms_f32(x, w):
    xf = x.astype(jnp.float32)
    return xf * lax.rsqrt(jnp.mean(xf * xf, axis=-1, keepdims=True) + EPS) * w.astype(jnp.float32)


def rmsnorm(x, w):
    return _rms_f32(x, w).astype(x.dtype)


def _split_points():
    return tuple(int(c) for c in np.cumsum(IN_SPLITS)[:-1])


def _complex_affine_combine(e1, e2):
    a1r, a1i, b1r, b1i = e1
    a2r, a2i, b2r, b2i = e2
    return (a2r * a1r - a2i * a1i,
            a2r * a1i + a2i * a1r,
            a2r * b1r - a2i * b1i + b2r,
            a2r * b1i + a2i * b1r + b2i)


def s5_branch(u, a_re, a_im, log_dt, b_re, b_im, c_re, c_im, d_skip, w_glu):
    f32 = jnp.float32
    bsz, seq, _ = u.shape
    ug = u.astype(f32).reshape(bsz, seq, N_SSM_GROUPS, SSM_GROUP)
    lam_r = jnp.minimum(a_re.astype(f32), -1e-4)
    lam_i = a_im.astype(f32)
    dt = jnp.exp(log_dt.astype(f32))[:, None]
    mag = jnp.exp(dt * lam_r)
    ab_r = mag * jnp.cos(dt * lam_i)
    ab_i = mag * jnp.sin(dt * lam_i)
    den = lam_r * lam_r + lam_i * lam_i
    f_r = ((ab_r - 1.0) * lam_r + ab_i * lam_i) / den
    f_i = (ab_i * lam_r - (ab_r - 1.0) * lam_i) / den
    br = b_re.astype(f32)
    bi = b_im.astype(f32)
    bb_r = f_r[..., None] * br - f_i[..., None] * bi
    bb_i = f_r[..., None] * bi + f_i[..., None] * br
    x_r = jnp.einsum('bsgc,gpc->bsgp', ug, bb_r)
    x_i = jnp.einsum('bsgc,gpc->bsgp', ug, bb_i)
    shape_a = (1, seq, N_SSM_GROUPS, SSM_STATE)
    a_seq_r = jnp.broadcast_to(ab_r, shape_a)
    a_seq_i = jnp.broadcast_to(ab_i, shape_a)
    _, _, h_r, h_i = lax.associative_scan(_complex_affine_combine, (a_seq_r, a_seq_i, x_r, x_i), axis=1)
    y = (jnp.einsum('bsgp,gcp->bsgc', h_r, c_re.astype(f32))
         - jnp.einsum('bsgp,gcp->bsgc', h_i, c_im.astype(f32)))
    y = y + d_skip.astype(f32).reshape(N_SSM_GROUPS, SSM_GROUP) * ug
    z = jax.nn.gelu(y.reshape(bsz, seq, D_SSM)).astype(u.dtype) @ w_glu
    val, gate = jnp.split(z, 2, axis=-1)
    return (val * jax.nn.sigmoid(gate)).astype(u.dtype)


def gla_branch(q, k, v, r, gate_lr, w_gate, b_gate, norm_w, w_br):
    f32 = jnp.float32
    bsz, seq, _ = q.shape
    n = seq // GLA_CHUNK
    cshape_k = (bsz, n, GLA_CHUNK, GLA_HEADS, GLA_DK)
    qf = q.astype(f32).reshape(cshape_k) * (GLA_DK ** -0.5)
    kf = k.astype(f32).reshape(cshape_k)
    vf = v.astype(f32).reshape(bsz, n, GLA_CHUNK, GLA_HEADS, GLA_DV)
    log_a = jax.nn.log_sigmoid(gate_lr.astype(f32) @ w_gate.astype(f32) + b_gate.astype(f32)) / GLA_GATE_NORM
    cum = jnp.cumsum(log_a.reshape(cshape_k), axis=2)
    last = cum[:, :, -1:]
    q_dec = qf * jnp.exp(cum)
    k_inv = kf * jnp.exp(-cum)
    k_tail = kf * jnp.exp(last - cum)
    tri = jnp.tril(jnp.ones((GLA_CHUNK, GLA_CHUNK), dtype=bool))
    scores = jnp.where(tri, jnp.einsum('bnchd,bnshd->bnhcs', q_dec, k_inv), 0.0)
    o_intra = jnp.einsum('bnhcs,bnshv->bnchv', scores, vf)

    def step(state, inp):
        qd, kt, vv, dl = inp
        o = jnp.einsum('bchd,bhdv->bchv', qd, state)
        state = dl[..., None] * state + jnp.einsum('bchd,bchv->bhdv', kt, vv)
        return state, o

    xs = (jnp.moveaxis(q_dec, 1, 0), jnp.moveaxis(k_tail, 1, 0), jnp.moveaxis(vf, 1, 0),
          jnp.moveaxis(jnp.exp(last[:, :, 0]), 1, 0))
    init = jnp.zeros((bsz, GLA_HEADS, GLA_DK, GLA_DV), f32)
    _, o_inter = lax.scan(step, init, xs)
    o = (o_intra + jnp.moveaxis(o_inter, 0, 1)).reshape(bsz, seq, GLA_HEADS, GLA_DV)
    o = _rms_f32(o, norm_w).reshape(bsz, seq, GLA_HEADS * GLA_DV) * jax.nn.silu(r.astype(f32))
    return o.astype(q.dtype) @ w_br


def diff_attn_branch(q, k, v, lam_q1, lam_k1, lam_q2, lam_k2, norm_w, w_br, lambda_init):
    f32 = jnp.float32
    bsz, seq, _ = q.shape
    q = q.reshape(bsz, seq, DIFF_HEADS, 2, DIFF_DK)
    k = k.reshape(bsz, seq, DIFF_HEADS, 2, DIFF_DK)
    v = v.reshape(bsz, seq, DIFF_HEADS, DIFF_DV)
    lam = (jnp.exp(jnp.sum(lam_q1.astype(f32) * lam_k1.astype(f32)))
           - jnp.exp(jnp.sum(lam_q2.astype(f32) * lam_k2.astype(f32))) + lambda_init)
    slopes = jnp.array(ALIBI_SLOPES, dtype=f32)[None, :, None, None, None]
    scale = DIFF_DK ** -0.5
    pos = jnp.arange(seq, dtype=jnp.int32)
    outs = []
    for blk in range(seq // Q_BLOCK):
        start = blk * Q_BLOCK
        end = start + Q_BLOCK
        qb = q[:, start:end]
        kb = k[:, :end]
        vb = v[:, :end]
        s = jnp.einsum('bqhmd,bkhmd->bhmqk', qb, kb).astype(f32) * scale
        rel = (pos[start:end, None] - pos[None, :end]).astype(f32)
        s = s - slopes * rel
        s = jnp.where(rel >= 0.0, s, -jnp.inf)
        p = jax.nn.softmax(s, axis=-1)
        w = p[:, :, 0] - lam * p[:, :, 1]
        outs.append(jnp.einsum('bhqk,bkhv->bqhv', w.astype(v.dtype), vb))
    o = jnp.concatenate(outs, axis=1)
    o = _rms_f32(o, norm_w) * (1.0 - lambda_init)
    return o.reshape(bsz, seq, DIFF_HEADS * DIFF_DV).astype(q.dtype) @ w_br


def conv_gated_mlp(h, w_up, conv_w, conv_b, w_down):
    seq = h.shape[1]
    u = h @ w_up
    pad = jnp.pad(u, ((0, 0), (CONV_W - 1, 0), (0, 0)))
    acc = conv_b
    for i in range(CONV_W):
        acc = acc + pad[:, i:i + seq] * conv_w[i]
    val, gate = jnp.split(acc, 2, axis=-1)
    return (jax.nn.silu(gate) * val) @ w_down


def setup_inputs(seed: int = 0) -> dict:
    key = jax.random.key(seed)
    ks = jax.random.split(key, 32)
    f32 = jnp.float32
    L = DEPTH
    G = N_SSM_GROUPS
    P = SSM_STATE

    def nrm(k, shape, scale):
        return jax.random.normal(k, shape, f32) * scale

    def gain(k, n):
        return 1.0 + nrm(k, (L, n), 0.02)

    return {
        'x': nrm(ks[0], (BATCH, SEQ, D_MODEL), 1.0),
        'norm_mix_w': gain(ks[1], D_MODEL),
        'w_in': nrm(ks[2], (L, D_MODEL, D_IN), D_MODEL ** -0.5),
        'ssm_a_re': -0.5 + nrm(ks[3], (L, G, P), 0.01),
        'ssm_a_im': jnp.pi * jnp.arange(P, dtype=f32) + nrm(ks[4], (L, G, P), 0.01),
        'ssm_log_dt': jax.random.uniform(ks[5], (L, G), f32, math.log(SSM_DT_MIN), math.log(SSM_DT_MAX)),
        'ssm_b_re': nrm(ks[6], (L, G, P, SSM_GROUP), (2 * SSM_GROUP) ** -0.5),
        'ssm_b_im': nrm(ks[7], (L, G, P, SSM_GROUP), (2 * SSM_GROUP) ** -0.5),
        'ssm_c_re': nrm(ks[8], (L, G, SSM_GROUP, P), (2 * P) ** -0.5),
        'ssm_c_im': nrm(ks[9], (L, G, SSM_GROUP, P), (2 * P) ** -0.5),
        'ssm_d': nrm(ks[10], (L, D_SSM), 1.0),
        'ssm_w_glu': nrm(ks[11], (L, D_SSM, 2 * D_MODEL), D_SSM ** -0.5),
        'gla_w_gate': nrm(ks[12], (L, GLA_GATE_RANK, GLA_HEADS * GLA_DK), GLA_GATE_RANK ** -0.5),
        'gla_b_gate': nrm(ks[13], (L, GLA_HEADS * GLA_DK), 0.1),
        'gla_norm_w': gain(ks[14], GLA_DV),
        'gla_w_br': nrm(ks[15], (L, GLA_HEADS * GLA_DV, D_MODEL), (GLA_HEADS * GLA_DV) ** -0.5),
        'diff_lam_q1': nrm(ks[16], (L, DIFF_DK), 0.1),
        'diff_lam_k1': nrm(ks[17], (L, DIFF_DK), 0.1),
        'diff_lam_q2': nrm(ks[18], (L, DIFF_DK), 0.1),
        'diff_lam_k2': nrm(ks[19], (L, DIFF_DK), 0.1),
        'diff_norm_w': gain(ks[20], DIFF_DV),
        'diff_w_br': nrm(ks[21], (L, DIFF_HEADS * DIFF_DV, D_MODEL), (DIFF_HEADS * DIFF_DV) ** -0.5),
        'w_out': nrm(ks[22], (L, D_MODEL, D_MODEL), D_MODEL ** -0.5),
        'norm_ffn_w': gain(ks[23], D_MODEL),
        'ffn_w_up': nrm(ks[24], (L, D_MODEL, 2 * D_FF), D_MODEL ** -0.5),
        'ffn_conv_w': nrm(ks[25], (L, CONV_W, 2 * D_FF), 0.5),
        'ffn_conv_b': nrm(ks[26], (L, 2 * D_FF), 0.01),
        'ffn_w_down': nrm(ks[27], (L, D_FF, D_MODEL), D_FF ** -0.5),
        'norm_final_w': 1.0 + nrm(ks[28], (D_MODEL,), 0.02),
    }


def reference(x, norm_mix_w, w_in, ssm_a_re, ssm_a_im, ssm_log_dt, ssm_b_re, ssm_b_im, ssm_c_re, ssm_c_im,
              ssm_d, ssm_w_glu, gla_w_gate, gla_b_gate, gla_norm_w, gla_w_br, diff_lam_q1, diff_lam_k1,
              diff_lam_q2, diff_lam_k2, diff_norm_w, diff_w_br, w_out, norm_ffn_w, ffn_w_up, ffn_conv_w,
              ffn_conv_b, ffn_w_down, norm_final_w):
    bsz, seq, _ = x.shape
    split_points = _split_points()
    h = x
    for l in range(DEPTH):
        lambda_init = 0.8 - 0.6 * math.exp(-0.3 * l)
        hn = rmsnorm(h, norm_mix_w[l])
        proj = hn @ w_in[l]
        (u_ssm, q_gla, k_gla, v_gla, r_gla, a_lr,
         q_diff, k_diff, v_diff, gates) = jnp.split(proj, split_points, axis=-1)
        y_a = s5_branch(u_ssm, ssm_a_re[l], ssm_a_im[l], ssm_log_dt[l], ssm_b_re[l], ssm_b_im[l],
                        ssm_c_re[l], ssm_c_im[l], ssm_d[l], ssm_w_glu[l])
        y_b = gla_branch(q_gla, k_gla, v_gla, r_gla, a_lr, gla_w_gate[l], gla_b_gate[l],
                         gla_norm_w[l], gla_w_br[l])
        y_c = diff_attn_branch(q_diff, k_diff, v_diff, diff_lam_q1[l], diff_lam_k1[l], diff_lam_q2[l],
                               diff_lam_k2[l], diff_norm_w[l], diff_w_br[l], lambda_init)
        g = jax.nn.sigmoid(gates.astype(jnp.float32)).reshape(bsz, seq, N_BRANCH, D_MODEL)
        merged = (g[:, :, 0] * y_a.astype(jnp.float32) + g[:, :, 1] * y_b.astype(jnp.float32)
                  + g[:, :, 2] * y_c.astype(jnp.float32))
        h = h + merged.astype(h.dtype) @ w_out[l]
        h = h + conv_gated_mlp(rmsnorm(h, norm_ffn_w[l]), ffn_w_up[l], ffn_conv_w[l],
                               ffn_conv_b[l], ffn_w_down[l])
    return rmsnorm(h, norm_final_w)
```

```python
import functools
import math

import jax
import jax.numpy as jnp
import numpy as np
from jax import lax
from jax.experimental import pallas as pl
from jax.experimental.pallas import tpu as pltpu

F32 = jnp.float32
BF16 = jnp.bfloat16

D_MODEL = 2048
DEPTH = 2
D_SSM = 1024
SSM_GROUP = 16
N_SSM_GROUPS = D_SSM // SSM_GROUP
SSM_STATE = 64
GLA_HEADS = 4
GLA_DK = 128
GLA_DV = 256
GLA_GATE_RANK = 16
GLA_GATE_NORM = 16.0
GLA_CHUNK = 32
DIFF_HEADS = 4
DIFF_DK = 128
DIFF_DV = 256
N_BRANCH = 3
D_FF = 5504
CONV_W = 3
EPS = 1e-6
ALIBI_SLOPES = tuple(2.0 ** (-8.0 * (h + 1) / DIFF_HEADS) for h in range(DIFF_HEADS))

LANES = 128
SUBLANES = 8
BF16_ROWS = 16

COL_U = 0
COL_QG = COL_U + D_SSM
COL_KG = COL_QG + GLA_HEADS * GLA_DK
COL_VG = COL_KG + GLA_HEADS * GLA_DK
COL_RG = COL_VG + GLA_HEADS * GLA_DV
COL_QD = COL_RG + GLA_HEADS * GLA_DV
COL_KD = COL_QD + DIFF_HEADS * 2 * DIFF_DK
COL_VD = COL_KD + DIFF_HEADS * 2 * DIFF_DK
COL_GT = COL_VD + DIFF_HEADS * DIFF_DV
N_MAIN = COL_GT + N_BRANCH * D_MODEL
ALR_ORIG = COL_QD

D_FF_PAD = 5632
FFN_TILE = 512
HALO = BF16_ROWS

VMEM_LIMIT = 56 * 1024 * 1024


def _cparams(n_axes):
    return pltpu.CompilerParams(
        dimension_semantics=("arbitrary",) * n_axes, vmem_limit_bytes=VMEM_LIMIT)


def _rms_rows(x, w):
    ms = jnp.mean(x * x, axis=-1, keepdims=True)
    return x * lax.rsqrt(ms + EPS) * w


def _inproj_kernel(h_ref, nw_ref, w_ref, walr_ref, o_ref, alr_ref, hn_ref):
    @pl.when(pl.program_id(1) == 0)
    def _():
        hn = _rms_rows(h_ref[...], nw_ref[...]).astype(BF16)
        hn_ref[...] = hn
        alr_ref[...] = jnp.dot(hn, walr_ref[...], preferred_element_type=F32).astype(alr_ref.dtype)

    o_ref[...] = jnp.dot(hn_ref[...], w_ref[...], preferred_element_type=F32).astype(o_ref.dtype)


def _inproj(h, nw, w_main, w_alr, tm, tn):
    t = h.shape[0]
    return pl.pallas_call(
        _inproj_kernel,
        grid=(t // tm, N_MAIN // tn),
        in_specs=[
            pl.BlockSpec((tm, D_MODEL), lambda i, j: (i, 0)),
            pl.BlockSpec((1, D_MODEL), lambda i, j: (0, 0)),
            pl.BlockSpec((D_MODEL, tn), lambda i, j: (0, j)),
            pl.BlockSpec((D_MODEL, LANES), lambda i, j: (0, 0)),
        ],
        out_specs=[
            pl.BlockSpec((tm, tn), lambda i, j: (i, j)),
            pl.BlockSpec((tm, LANES), lambda i, j: (i, 0)),
        ],
        out_shape=[
            jax.ShapeDtypeStruct((t, N_MAIN), BF16),
            jax.ShapeDtypeStruct((t, LANES), BF16),
        ],
        scratch_shapes=[pltpu.VMEM((tm, D_MODEL), BF16)],
        compiler_params=_cparams(2),
        name="inproj",
    )(h, nw, w_main, w_alr)


S5_CHUNKS = D_SSM // LANES
S5_SLABS = N_SSM_GROUPS * SSM_STATE // LANES
S5_SLABS_PER_CHUNK = S5_SLABS // S5_CHUNKS
S5_VREGS = S5_SLABS // SUBLANES


def _s5_kernel(u_ref, bw_ref, cw_ref, are_ref, aim_ref, d_ref, o_ref, xs_ref, st_ref, *, tc, pitch):
    @pl.when(pl.program_id(1) == 0)
    def _():
        st_ref[...] = jnp.zeros_like(st_ref)

    half = S5_SLABS_PER_CHUNK * LANES
    for c in range(S5_CHUNKS):
        x = jnp.dot(u_ref[:, c * LANES:(c + 1) * LANES], bw_ref[c], preferred_element_type=F32)
        for q in range(S5_SLABS_PER_CHUNK):
            r = c * S5_SLABS_PER_CHUNK + q
            xs_ref[pl.ds(r * pitch, tc), :] = x[:, q * LANES:(q + 1) * LANES]
            xs_ref[pl.ds((S5_SLABS + r) * pitch, tc), :] = x[:, half + q * LANES:half + (q + 1) * LANES]

    a_re = [are_ref[k * SUBLANES:(k + 1) * SUBLANES, :] for k in range(S5_VREGS)]
    a_im = [aim_ref[k * SUBLANES:(k + 1) * SUBLANES, :] for k in range(S5_VREGS)]

    def step(t, carry):
        out = []
        for k in range(S5_VREGS):
            re, im = carry[2 * k], carry[2 * k + 1]
            idx_re = pl.ds(k * SUBLANES * pitch + t, SUBLANES, stride=pitch)
            idx_im = pl.ds((S5_SLABS + k * SUBLANES) * pitch + t, SUBLANES, stride=pitch)
            nre = a_re[k] * re - a_im[k] * im + xs_ref[idx_re, :]
            nim = a_re[k] * im + a_im[k] * re + xs_ref[idx_im, :]
            xs_ref[idx_re, :] = nre
            xs_ref[idx_im, :] = nim
            out += [nre, nim]
        return tuple(out)

    init = []
    for k in range(S5_VREGS):
        init += [st_ref[k * SUBLANES:(k + 1) * SUBLANES, :],
                 st_ref[S5_SLABS + k * SUBLANES:S5_SLABS + (k + 1) * SUBLANES, :]]
    fin = lax.fori_loop(0, tc, step, tuple(init), unroll=8)
    for k in range(S5_VREGS):
        st_ref[k * SUBLANES:(k + 1) * SUBLANES, :] = fin[2 * k]
        st_ref[S5_SLABS + k * SUBLANES:S5_SLABS + (k + 1) * SUBLANES, :] = fin[2 * k + 1]

    for c in range(S5_CHUNKS):
        parts = []
        for base in (0, S5_SLABS):
            for q in range(S5_SLABS_PER_CHUNK):
                r = base + c * S5_SLABS_PER_CHUNK + q
                parts.append(xs_ref[pl.ds(r * pitch, tc), :].astype(BF16))
        hcat = jnp.concatenate(parts, axis=1)
        y = jnp.dot(hcat, cw_ref[c], preferred_element_type=F32)
        lanes = slice(c * LANES, (c + 1) * LANES)
        y = y + d_ref[:, lanes] * u_ref[:, lanes].astype(F32)
        o_ref[:, lanes] = jax.nn.gelu(y).astype(o_ref.dtype)


def _s5_weights(a_re, a_im, log_dt, b_re, b_im, c_re, c_im):
    lam_r = jnp.minimum(a_re.astype(F32), -1e-4)
    lam_i = a_im.astype(F32)
    dt = jnp.exp(log_dt.astype(F32))[:, None]
    mag = jnp.exp(dt * lam_r)
    ab_r = mag * jnp.cos(dt * lam_i)
    ab_i = mag * jnp.sin(dt * lam_i)
    den = lam_r * lam_r + lam_i * lam_i
    f_r = ((ab_r - 1.0) * lam_r + ab_i * lam_i) / den
    f_i = (ab_i * lam_r - (ab_r - 1.0) * lam_i) / den
    br = b_re.astype(F32)
    bi = b_im.astype(F32)
    bb_r = f_r[..., None] * br - f_i[..., None] * bi
    bb_i = f_r[..., None] * bi + f_i[..., None] * br
    gpc = LANES // SSM_GROUP
    eye = jnp.eye(gpc, dtype=F32)

    def drive(bb):
        bb = bb.reshape(S5_CHUNKS, gpc, SSM_STATE, SSM_GROUP)
        m = jnp.einsum('cgpk,gh->cgkhp', bb, eye)
        return m.reshape(S5_CHUNKS, LANES, gpc * SSM_STATE)

    def readout(cc):
        cc = cc.astype(F32).reshape(S5_CHUNKS, gpc, SSM_GROUP, SSM_STATE)
        m = jnp.einsum('cgkp,gh->cgphk', cc, eye)
        return m.reshape(S5_CHUNKS, gpc * SSM_STATE, LANES)

    bw = jnp.concatenate([drive(bb_r), drive(bb_i)], axis=2).astype(BF16)
    cw = jnp.concatenate([readout(c_re), -readout(c_im)], axis=1).astype(BF16)
    return bw, cw, ab_r.reshape(S5_SLABS, LANES), ab_i.reshape(S5_SLABS, LANES)


def _s5(proj, bw, cw, are, aim, d_skip, bsz, seq, tc):
    nt = seq // tc
    pitch = tc + SUBLANES
    kern = functools.partial(_s5_kernel, tc=tc, pitch=pitch)
    const3 = lambda b, i: (0, 0, 0)
    const2 = lambda b, i: (0, 0)
    return pl.pallas_call(
        kern,
        grid=(bsz, nt),
        in_specs=[
            pl.BlockSpec((tc, D_SSM), lambda b, i: (b * nt + i, COL_U // D_SSM)),
            pl.BlockSpec(bw.shape, const3),
            pl.BlockSpec(cw.shape, const3),
            pl.BlockSpec(are.shape, const2),
            pl.BlockSpec(aim.shape, const2),
            pl.BlockSpec((1, D_SSM), const2),
        ],
        out_specs=pl.BlockSpec((tc, D_SSM), lambda b, i: (b * nt + i, 0)),
        out_shape=jax.ShapeDtypeStruct((bsz * seq, D_SSM), BF16),
        scratch_shapes=[
            pltpu.VMEM((2 * S5_SLABS * pitch, LANES), F32),
            pltpu.VMEM((2 * S5_SLABS, LANES), F32),
        ],
        compiler_params=_cparams(2),
        name="s5",
    )(proj, bw, cw, are, aim, d_skip)


def _split3(x):
    hi = x.astype(BF16)
    r1 = x - hi.astype(F32)
    mid = r1.astype(BF16)
    lo = (r1 - mid.astype(F32)).astype(BF16)
    return hi, mid, lo


def _gla_kernel(q_ref, k_ref, v_ref, r_ref, alr_ref, wg_ref, bg_ref, nw_ref, mt_ref, mo_ref,
                o_ref, cum_ref, last_ref, st_ref, *, tc):
    @pl.when(pl.program_id(1) == 0)
    def _():
        st_ref[...] = jnp.zeros_like(st_ref)

    z = jnp.dot(alr_ref[...], wg_ref[...], preferred_element_type=F32) + bg_ref[...]
    log_a = (jnp.minimum(z, 0.0) - jnp.log1p(jnp.exp(-jnp.abs(z)))) * (1.0 / GLA_GATE_NORM)
    parts = _split3(log_a)
    mt = mt_ref[...]
    mo = mo_ref[...]
    cum_ref[...] = sum(jnp.dot(mt, p, preferred_element_type=F32) for p in parts)
    last_ref[...] = sum(jnp.dot(mo, p, preferred_element_type=F32) for p in parts)

    rows_i = lax.broadcasted_iota(jnp.int32, (GLA_CHUNK, GLA_CHUNK), 0)
    cols_i = lax.broadcasted_iota(jnp.int32, (GLA_CHUNK, GLA_CHUNK), 1)
    tri = rows_i >= cols_i
    scale = GLA_DK ** -0.5
    nt_dims = (((1,), (1,)), ((), ()))
    tn_dims = (((0,), (0,)), ((), ()))

    def chunk(n, carry):
        rows = pl.ds(pl.multiple_of(n * GLA_CHUNK, GLA_CHUNK), GLA_CHUNK)
        for hd in range(GLA_HEADS):
            kc = slice(hd * GLA_DK, (hd + 1) * GLA_DK)
            vc = slice(hd * GLA_DV, (hd + 1) * GLA_DV)
            cum = cum_ref[rows, kc]
            last = last_ref[rows, kc]
            qf = q_ref[rows, kc].astype(F32) * scale
            kf = k_ref[rows, kc].astype(F32)
            qd = (qf * jnp.exp(cum)).astype(BF16)
            ki = (kf * jnp.exp(-cum)).astype(BF16)
            kt = (kf * jnp.exp(last - cum)).astype(BF16)
            dl = jnp.exp(last[0:1, :])
            vv = v_ref[rows, vc]
            sc = lax.dot_general(qd, ki, nt_dims, preferred_element_type=F32)
            sc = jnp.where(tri, sc, 0.0).astype(BF16)
            o = jnp.dot(sc, vv, preferred_element_type=F32)
            st = st_ref[hd]
            o = o + lax.dot_general(qd, st.astype(BF16), nt_dims, preferred_element_type=F32)
            st_ref[hd] = dl * st + lax.dot_general(vv, kt, tn_dims, preferred_element_type=F32)
            on = _rms_rows(o, nw_ref[...])
            rr = r_ref[rows, vc].astype(F32)
            o_ref[rows, vc] = (on * (rr * jax.nn.sigmoid(rr))).astype(o_ref.dtype)
        return carry

    lax.fori_loop(0, tc // GLA_CHUNK, chunk, 0)


def _gla(proj, alr, wg, bg, nw, bsz, seq, tc):
    nt = seq // tc
    hk = GLA_HEADS * GLA_DK
    hv = GLA_HEADS * GLA_DV
    pos = np.arange(tc)
    same = (pos[:, None] // GLA_CHUNK) == (pos[None, :] // GLA_CHUNK)
    mt = jnp.asarray(same & (pos[:, None] >= pos[None, :]), BF16)
    mo = jnp.asarray(same, BF16)
    kern = functools.partial(_gla_kernel, tc=tc)
    const2 = lambda b, i: (0, 0)
    row = lambda col: (lambda b, i: (b * nt + i, col))
    return pl.pallas_call(
        kern,
        grid=(bsz, nt),
        in_specs=[
            pl.BlockSpec((tc, hk), row(COL_QG // hk)),
            pl.BlockSpec((tc, hk), row(COL_KG // hk)),
            pl.BlockSpec((tc, hv), row(COL_VG // hv)),
            pl.BlockSpec((tc, hv), row(COL_RG // hv)),
            pl.BlockSpec((tc, LANES), row(0)),
            pl.BlockSpec((LANES, hk), const2),
            pl.BlockSpec((1, hk), const2),
            pl.BlockSpec((1, GLA_DV), const2),
            pl.BlockSpec((tc, tc), const2),
            pl.BlockSpec((tc, tc), const2),
        ],
        out_specs=pl.BlockSpec((tc, hv), row(0)),
        out_shape=jax.ShapeDtypeStruct((bsz * seq, hv), BF16),
        scratch_shapes=[
            pltpu.VMEM((tc, hk), F32),
            pltpu.VMEM((tc, hk), F32),
            pltpu.VMEM((GLA_HEADS, GLA_DV, GLA_DK), F32),
        ],
        compiler_params=_cparams(2),
        name="gla",
    )(proj, proj, proj, proj, alr, wg, bg, nw, mt, mo)


def _attn_kernel(q_ref, k_ref, v_ref, lq1_ref, lk1_ref, lq2_ref, lk2_ref, nw_ref, o_ref, acc_ref,
                 *, tq, lambda_init):
    head = pl.program_id(1)
    qi = pl.program_id(2)
    slope = jnp.float32(ALIBI_SLOPES[-1])
    for hd in range(DIFF_HEADS - 1):
        slope = jnp.where(head == hd, jnp.float32(ALIBI_SLOPES[hd]), slope)
    scale = DIFF_DK ** -0.5
    nt_dims = (((1,), (1,)), ((), ()))
    delta = (lax.broadcasted_iota(jnp.int32, (tq, tq), 0)
             - lax.broadcasted_iota(jnp.int32, (tq, tq), 1)).astype(F32)
    qs = [q_ref[:, m * DIFF_DK:(m + 1) * DIFF_DK] for m in range(2)]
    acc_ref[...] = jnp.zeros_like(acc_ref)

    def tile(j, carry, masked):
        rows = pl.ds(pl.multiple_of(j * tq, tq), tq)
        kb = k_ref[rows, :]
        vb = v_ref[rows, :]
        rel = delta + ((qi - j) * tq).astype(F32)
        bias = slope * rel
        out = []
        for m in range(2):
            m_prev, l_prev = carry[2 * m], carry[2 * m + 1]
            s = lax.dot_general(qs[m], kb[:, m * DIFF_DK:(m + 1) * DIFF_DK], nt_dims,
                                preferred_element_type=F32) * scale - bias
            if masked:
                s = jnp.where(rel >= 0.0, s, -jnp.inf)
            m_new = jnp.maximum(m_prev, jnp.max(s, axis=-1, keepdims=True))
            alpha = jnp.exp(m_prev - m_new)
            p = jnp.exp(s - m_new)
            l_new = alpha * l_prev + jnp.sum(p, axis=-1, keepdims=True)
            acc_ref[m] = alpha * acc_ref[m] + jnp.dot(p.astype(BF16), vb, preferred_element_type=F32)
            out += [m_new, l_new]
        return tuple(out)

    neg = jnp.full((tq, 1), -jnp.inf, F32)
    zero = jnp.zeros((tq, 1), F32)
    carry = lax.fori_loop(0, qi, lambda j, c: tile(j, c, False), (neg, zero, neg, zero))
    _, l1, _, l2 = tile(qi, carry, True)

    lam = (jnp.exp(jnp.sum(lq1_ref[...] * lk1_ref[...], axis=-1, keepdims=True))
           - jnp.exp(jnp.sum(lq2_ref[...] * lk2_ref[...], axis=-1, keepdims=True)) + lambda_init)
    o = acc_ref[0] / l1 - lam * (acc_ref[1] / l2)
    o = _rms_rows(o, nw_ref[...]) * (1.0 - lambda_init)
    o_ref[...] = o.astype(o_ref.dtype)


def _attn(proj, lq1, lk1, lq2, lk2, nw, bsz, seq, tq, lambda_init):
    nq = seq // tq
    w2 = 2 * DIFF_DK
    kern = functools.partial(_attn_kernel, tq=tq, lambda_init=lambda_init)
    const2 = lambda b, h, i: (0, 0)
    return pl.pallas_call(
        kern,
        grid=(bsz, DIFF_HEADS, nq),
        in_specs=[
            pl.BlockSpec((tq, w2), lambda b, h, i: (b * nq + i, COL_QD // w2 + h)),
            pl.BlockSpec((seq, w2), lambda b, h, i: (b, COL_KD // w2 + h)),
            pl.BlockSpec((seq, DIFF_DV), lambda b, h, i: (b, COL_VD // DIFF_DV + h)),
            pl.BlockSpec((1, DIFF_DK), const2),
            pl.BlockSpec((1, DIFF_DK), const2),
            pl.BlockSpec((1, DIFF_DK), const2),
            pl.BlockSpec((1, DIFF_DK), const2),
            pl.BlockSpec((1, DIFF_DV), const2),
        ],
        out_specs=pl.BlockSpec((tq, DIFF_DV), lambda b, h, i: (b * nq + i, h)),
        out_shape=jax.ShapeDtypeStruct((bsz * seq, DIFF_HEADS * DIFF_DV), BF16),
        scratch_shapes=[pltpu.VMEM((2, tq, DIFF_DV), F32)],
        compiler_params=_cparams(3),
        name="diffattn",
    )(proj, proj, proj, lq1, lk1, lq2, lk2, nw)


def _merge_kernel(gy_ref, ob_ref, oc_ref, g0_ref, g1_ref, g2_ref, wgv_ref, wgg_ref, wb_ref, wc_ref,
                  wo_ref, h_ref, o_ref, acc_ref):
    j = pl.program_id(1)

    @pl.when(j == 0)
    def _():
        acc_ref[...] = jnp.zeros_like(acc_ref)

    gy = gy_ref[...]
    val = jnp.dot(gy, wgv_ref[...], preferred_element_type=F32)
    gate = jnp.dot(gy, wgg_ref[...], preferred_element_type=F32)
    ya = val * jax.nn.sigmoid(gate)
    yb = jnp.dot(ob_ref[...], wb_ref[...], preferred_element_type=F32)
    yc = jnp.dot(oc_ref[...], wc_ref[...], preferred_element_type=F32)
    merged = (jax.nn.sigmoid(g0_ref[...].astype(F32)) * ya
              + jax.nn.sigmoid(g1_ref[...].astype(F32)) * yb
              + jax.nn.sigmoid(g2_ref[...].astype(F32)) * yc)
    acc_ref[...] += jnp.dot(merged.astype(BF16), wo_ref[...], preferred_element_type=F32)

    @pl.when(j == pl.num_programs(1) - 1)
    def _():
        o_ref[...] = h_ref[...] + acc_ref[...]


def _merge(gy, ob, oc, proj, w_glu, wbr_g, wbr_d, w_out, h, tm, tn):
    t = h.shape[0]
    nj = D_MODEL // tn
    gt0 = COL_GT // tn
    left = lambda i, j: (i, 0)
    return pl.pallas_call(
        _merge_kernel,
        grid=(t // tm, nj),
        in_specs=[
            pl.BlockSpec((tm, D_SSM), left),
            pl.BlockSpec((tm, GLA_HEADS * GLA_DV), left),
            pl.BlockSpec((tm, DIFF_HEADS * DIFF_DV), left),
            pl.BlockSpec((tm, tn), lambda i, j: (i, gt0 + j)),
            pl.BlockSpec((tm, tn), lambda i, j: (i, gt0 + nj + j)),
            pl.BlockSpec((tm, tn), lambda i, j: (i, gt0 + 2 * nj + j)),
            pl.BlockSpec((D_SSM, tn), lambda i, j: (0, j)),
            pl.BlockSpec((D_SSM, tn), lambda i, j: (0, nj + j)),
            pl.BlockSpec((GLA_HEADS * GLA_DV, tn), lambda i, j: (0, j)),
            pl.BlockSpec((DIFF_HEADS * DIFF_DV, tn), lambda i, j: (0, j)),
            pl.BlockSpec((tn, D_MODEL), lambda i, j: (j, 0)),
            pl.BlockSpec((tm, D_MODEL), left),
        ],
        out_specs=pl.BlockSpec((tm, D_MODEL), left),
        out_shape=jax.ShapeDtypeStruct((t, D_MODEL), F32),
        scratch_shapes=[pltpu.VMEM((tm, D_MODEL), F32)],
        compiler_params=_cparams(2),
        name="merge",
    )(gy, ob, oc, proj, proj, proj, w_glu, w_glu, wbr_g, wbr_d, w_out, h)


def _ffn_kernel(h_ref, halo_ref, nw_ref, wv_ref, wg_ref, cwv_ref, cwg_ref, cbv_ref, cbg_ref, wd_ref,
                fnw_ref, o_ref, hn_ref, acc_ref, *, tm, tiles_per_seq, final_norm):
    i = pl.program_id(0)
    j = pl.program_id(1)

    @pl.when(j == 0)
    def _():
        keep = (i % tiles_per_seq != 0).astype(F32)
        hn_ref[0:HALO, :] = (_rms_rows(halo_ref[...], nw_ref[...]) * keep).astype(BF16)
        hn_ref[HALO:, :] = _rms_rows(h_ref[...], nw_ref[...]).astype(BF16)
        acc_ref[...] = jnp.zeros_like(acc_ref)

    hn = hn_ref[...]

    def conv(w_ref, cw_ref, cb_ref):
        u = jnp.dot(hn, w_ref[...], preferred_element_type=F32)
        acc = cb_ref[...]
        for tap in range(CONV_W):
            lo = HALO - (CONV_W - 1) + tap
            acc = acc + u[lo:lo + tm, :] * cw_ref[tap:tap + 1, :]
        return acc

    val = conv(wv_ref, cwv_ref, cbv_ref)
    gate = conv(wg_ref, cwg_ref, cbg_ref)
    act = (gate * jax.nn.sigmoid(gate) * val).astype(BF16)
    acc_ref[...] += jnp.dot(act, wd_ref[...], preferred_element_type=F32)

    @pl.when(j == pl.num_programs(1) - 1)
    def _():
        out = h_ref[...] + acc_ref[...]
        if final_norm:
            out = _rms_rows(out, fnw_ref[...])
        o_ref[...] = out


def _ffn(h, nw, wv, wg, cwv, cwg, cbv, cbg, wd, fnw, seq, tm, final_norm):
    t = h.shape[0]
    tf = FFN_TILE
    kern = functools.partial(_ffn_kernel, tm=tm, tiles_per_seq=seq // tm, final_norm=final_norm)
    halo_blocks = tm // HALO
    const2 = lambda i, j: (0, 0)
    col = lambda i, j: (0, j)
    return pl.pallas_call(
        kern,
        grid=(t // tm, D_FF_PAD // tf),
        in_specs=[
            pl.BlockSpec((tm, D_MODEL), lambda i, j: (i, 0)),
            pl.BlockSpec((HALO, D_MODEL), lambda i, j: (jnp.maximum(i * halo_blocks - 1, 0), 0)),
            pl.BlockSpec((1, D_MODEL), const2),
            pl.BlockSpec((D_MODEL, tf), col),
            pl.BlockSpec((D_MODEL, tf), col),
            pl.BlockSpec((CONV_W, tf), col),
            pl.BlockSpec((CONV_W, tf), col),
            pl.BlockSpec((1, tf), col),
            pl.BlockSpec((1, tf), col),
            pl.BlockSpec((tf, D_MODEL), lambda i, j: (j, 0)),
            pl.BlockSpec((1, D_MODEL), const2),
        ],
        out_specs=pl.BlockSpec((tm, D_MODEL), lambda i, j: (i, 0)),
        out_shape=jax.ShapeDtypeStruct((t, D_MODEL), F32),
        scratch_shapes=[
            pltpu.VMEM((HALO + tm, D_MODEL), BF16),
            pltpu.VMEM((tm, D_MODEL), F32),
        ],
        compiler_params=_cparams(2),
        name="ffn",
    )(h, h, nw, wv, wg, cwv, cwg, cbv, cbg, wd, fnw)


def _pad_cols(a, n):
    return jnp.pad(a, ((0, 0), (0, n - a.shape[1])))


def _tile(n, pref):
    return pref if n % pref == 0 else n


def kernel(x, norm_mix_w, w_in, ssm_a_re, ssm_a_im, ssm_log_dt, ssm_b_re, ssm_b_im, ssm_c_re, ssm_c_im, ssm_d, ssm_w_glu, gla_w_gate, gla_b_gate, gla_norm_w, gla_w_br, diff_lam_q1, diff_lam_k1, diff_lam_q2, diff_lam_k2, diff_norm_w, diff_w_br, w_out, norm_ffn_w, ffn_w_up, ffn_conv_w, ffn_conv_b, ffn_w_down, norm_final_w):
    bsz, seq, _ = x.shape
    t = bsz * seq
    h = x.reshape(t, D_MODEL).astype(F32)
    row = lambda v: v.reshape(1, -1).astype(F32)

    for l in range(DEPTH):
        lambda_init = 0.8 - 0.6 * math.exp(-0.3 * l)
        w = w_in[l]
        w_main = jnp.concatenate([w[:, :ALR_ORIG], w[:, ALR_ORIG + GLA_GATE_RANK:]], axis=1).astype(BF16)
        w_alr = _pad_cols(w[:, ALR_ORIG:ALR_ORIG + GLA_GATE_RANK], LANES).astype(BF16)
        proj, alr = _inproj(h, row(norm_mix_w[l]), w_main, w_alr, _tile(t, 1024), 1024)

        bw, cw, are, aim = _s5_weights(ssm_a_re[l], ssm_a_im[l], ssm_log_dt[l], ssm_b_re[l],
                                       ssm_b_im[l], ssm_c_re[l], ssm_c_im[l])
        gy = _s5(proj, bw, cw, are, aim, row(ssm_d[l]), bsz, seq, _tile(seq, 256))

        wg = jnp.pad(gla_w_gate[l], ((0, LANES - GLA_GATE_RANK), (0, 0))).astype(BF16)
        ob = _gla(proj, alr, wg, row(gla_b_gate[l]), row(gla_norm_w[l]), bsz, seq, _tile(seq, 256))

        oc = _attn(proj, row(diff_lam_q1[l]), row(diff_lam_k1[l]), row(diff_lam_q2[l]),
                   row(diff_lam_k2[l]), row(diff_norm_w[l]), bsz, seq, _tile(seq, 512), lambda_init)

        h = _merge(gy, ob, oc, proj, ssm_w_glu[l].astype(BF16), gla_w_br[l].astype(BF16),
                   diff_w_br[l].astype(BF16), w_out[l].astype(BF16), h, _tile(t, 512), 512)

        wup = ffn_w_up[l]
        cwf = ffn_conv_w[l].astype(F32)
        cbf = ffn_conv_b[l].reshape(1, -1).astype(F32)
        h = _ffn(h, row(norm_ffn_w[l]),
                 _pad_cols(wup[:, :D_FF], D_FF_PAD).astype(BF16),
                 _pad_cols(wup[:, D_FF:], D_FF_PAD).astype(BF16),
                 _pad_cols(cwf[:, :D_FF], D_FF_PAD), _pad_cols(cwf[:, D_FF:], D_FF_PAD),
                 _pad_cols(cbf[:, :D_FF], D_FF_PAD), _pad_cols(cbf[:, D_FF:], D_FF_PAD),
                 jnp.pad(ffn_w_down[l], ((0, D_FF_PAD - D_FF), (0, 0))).astype(BF16),
                 row(norm_final_w), seq, _tile(seq, 512), l == DEPTH - 1)

    return h.reshape(bsz, seq, D_MODEL).astype(x.dtype)
```

```python
import functools
import math

import jax
import jax.numpy as jnp
import numpy as np
from jax import lax
from jax.experimental import pallas as pl
from jax.experimental.pallas import tpu as pltpu

F32 = jnp.float32
BF16 = jnp.bfloat16

D_MODEL = 2048
DEPTH = 2
D_SSM = 1024
SSM_GROUP = 16
N_SSM_GROUPS = D_SSM // SSM_GROUP
SSM_STATE = 64
GLA_HEADS = 4
GLA_DK = 128
GLA_DV = 256
GLA_GATE_RANK = 16
GLA_GATE_NORM = 16.0
GLA_CHUNK = 32
DIFF_HEADS = 4
DIFF_DK = 128
DIFF_DV = 256
N_BRANCH = 3
D_FF = 5504
CONV_W = 3
EPS = 1e-6
ALIBI_SLOPES = tuple(2.0 ** (-8.0 * (h + 1) / DIFF_HEADS) for h in range(DIFF_HEADS))
LOG2E = math.log2(math.e)

LANES = 128
SUBLANES = 8
BF16_ROWS = 16

COL_U = 0
COL_QG = COL_U + D_SSM
COL_KG = COL_QG + GLA_HEADS * GLA_DK
COL_VG = COL_KG + GLA_HEADS * GLA_DK
COL_RG = COL_VG + GLA_HEADS * GLA_DV
COL_QD = COL_RG + GLA_HEADS * GLA_DV
COL_KD = COL_QD + DIFF_HEADS * 2 * DIFF_DK
COL_VD = COL_KD + DIFF_HEADS * 2 * DIFF_DK
COL_GT = COL_VD + DIFF_HEADS * DIFF_DV
N_MAIN = COL_GT + N_BRANCH * D_MODEL
ALR_ORIG = COL_QD

D_FF_PAD = 5632
FFN_TILE = 512
HALO = BF16_ROWS

VMEM_LIMIT = 56 * 1024 * 1024


def _cparams(n_axes, flags=None):
    return pltpu.CompilerParams(
        dimension_semantics=("arbitrary",) * n_axes, vmem_limit_bytes=VMEM_LIMIT, flags=flags)


def _rms_rows(x, w):
    ms = jnp.mean(x * x, axis=-1, keepdims=True)
    return x * lax.rsqrt(ms + EPS) * w


def _inproj_kernel(h_ref, nw_ref, w_ref, walr_ref, cs_ref, o_ref, alr_ref, hn_ref):
    @pl.when(pl.program_id(1) == 0)
    def _():
        hn = _rms_rows(h_ref[...], nw_ref[...]).astype(BF16)
        hn_ref[...] = hn
        alr_ref[...] = jnp.dot(hn, walr_ref[...], preferred_element_type=F32).astype(alr_ref.dtype)

    acc = jnp.dot(hn_ref[...], w_ref[...], preferred_element_type=F32)
    o_ref[...] = (acc * cs_ref[...]).astype(o_ref.dtype)


def _inproj(h, nw, w_main, w_alr, tm, tn):
    t = h.shape[0]
    cs = jnp.ones((1, N_MAIN), F32).at[:, COL_QD:COL_KD].set(DIFF_DK ** -0.5 * LOG2E)
    return pl.pallas_call(
        _inproj_kernel,
        grid=(t // tm, N_MAIN // tn),
        in_specs=[
            pl.BlockSpec((tm, D_MODEL), lambda i, j: (i, 0)),
            pl.BlockSpec((1, D_MODEL), lambda i, j: (0, 0)),
            pl.BlockSpec((D_MODEL, tn), lambda i, j: (0, j)),
            pl.BlockSpec((D_MODEL, LANES), lambda i, j: (0, 0)),
            pl.BlockSpec((1, tn), lambda i, j: (0, j)),
        ],
        out_specs=[
            pl.BlockSpec((tm, tn), lambda i, j: (i, j)),
            pl.BlockSpec((tm, LANES), lambda i, j: (i, 0)),
        ],
        out_shape=[
            jax.ShapeDtypeStruct((t, N_MAIN), BF16),
            jax.ShapeDtypeStruct((t, LANES), BF16),
        ],
        scratch_shapes=[pltpu.VMEM((tm, D_MODEL), BF16)],
        compiler_params=_cparams(2),
        name="inproj",
    )(h, nw, w_main, w_alr, cs)


S5_CHUNKS = D_SSM // LANES
S5_SLABS = N_SSM_GROUPS * SSM_STATE // LANES
S5_SLABS_PER_CHUNK = S5_SLABS // S5_CHUNKS
S5_VREGS = S5_SLABS // SUBLANES


def _s5_kernel(u_ref, bw_ref, cw_ref, are_ref, aim_ref, d_ref, o_ref, xs_ref, st_ref, *, tc, pitch):
    @pl.when(pl.program_id(1) == 0)
    def _():
        st_ref[...] = jnp.zeros_like(st_ref)

    half = S5_SLABS_PER_CHUNK * LANES
    for c in range(S5_CHUNKS):
        x = jnp.dot(u_ref[:, c * LANES:(c + 1) * LANES], bw_ref[c], preferred_element_type=F32)
        for q in range(S5_SLABS_PER_CHUNK):
            r = c * S5_SLABS_PER_CHUNK + q
            xs_ref[pl.ds(r * pitch, tc), :] = x[:, q * LANES:(q + 1) * LANES]
            xs_ref[pl.ds((S5_SLABS + r) * pitch, tc), :] = x[:, half + q * LANES:half + (q + 1) * LANES]

    a_re = [are_ref[k * SUBLANES:(k + 1) * SUBLANES, :] for k in range(S5_VREGS)]
    a_im = [aim_ref[k * SUBLANES:(k + 1) * SUBLANES, :] for k in range(S5_VREGS)]

    def step(t, carry):
        out = []
        for k in range(S5_VREGS):
            re, im = carry[2 * k], carry[2 * k + 1]
            idx_re = pl.ds(k * SUBLANES * pitch + t, SUBLANES, stride=pitch)
            idx_im = pl.ds((S5_SLABS + k * SUBLANES) * pitch + t, SUBLANES, stride=pitch)
            nre = a_re[k] * re - a_im[k] * im + xs_ref[idx_re, :]
            nim = a_re[k] * im + a_im[k] * re + xs_ref[idx_im, :]
            xs_ref[idx_re, :] = nre
            xs_ref[idx_im, :] = nim
            out += [nre, nim]
        return tuple(out)

    init = []
    for k in range(S5_VREGS):
        init += [st_ref[k * SUBLANES:(k + 1) * SUBLANES, :],
                 st_ref[S5_SLABS + k * SUBLANES:S5_SLABS + (k + 1) * SUBLANES, :]]
    fin = lax.fori_loop(0, tc, step, tuple(init), unroll=8)
    for k in range(S5_VREGS):
        st_ref[k * SUBLANES:(k + 1) * SUBLANES, :] = fin[2 * k]
        st_ref[S5_SLABS + k * SUBLANES:S5_SLABS + (k + 1) * SUBLANES, :] = fin[2 * k + 1]

    for c in range(S5_CHUNKS):
        parts = []
        for base in (0, S5_SLABS):
            for q in range(S5_SLABS_PER_CHUNK):
                r = base + c * S5_SLABS_PER_CHUNK + q
                parts.append(xs_ref[pl.ds(r * pitch, tc), :].astype(BF16))
        hcat = jnp.concatenate(parts, axis=1)
        y = jnp.dot(hcat, cw_ref[c], preferred_element_type=F32)
        lanes = slice(c * LANES, (c + 1) * LANES)
        y = y + d_ref[:, lanes] * u_ref[:, lanes].astype(F32)
        o_ref[:, lanes] = jax.nn.gelu(y).astype(o_ref.dtype)


def _s5_weights(a_re, a_im, log_dt, b_re, b_im, c_re, c_im):
    lam_r = jnp.minimum(a_re.astype(F32), -1e-4)
    lam_i = a_im.astype(F32)
    dt = jnp.exp(log_dt.astype(F32))[:, None]
    mag = jnp.exp(dt * lam_r)
    ab_r = mag * jnp.cos(dt * lam_i)
    ab_i = mag * jnp.sin(dt * lam_i)
    den = lam_r * lam_r + lam_i * lam_i
    f_r = ((ab_r - 1.0) * lam_r + ab_i * lam_i) / den
    f_i = (ab_i * lam_r - (ab_r - 1.0) * lam_i) / den
    br = b_re.astype(F32)
    bi = b_im.astype(F32)
    bb_r = f_r[..., None] * br - f_i[..., None] * bi
    bb_i = f_r[..., None] * bi + f_i[..., None] * br
    gpc = LANES // SSM_GROUP
    eye = jnp.eye(gpc, dtype=F32)

    def drive(bb):
        bb = bb.reshape(S5_CHUNKS, gpc, SSM_STATE, SSM_GROUP)
        m = jnp.einsum('cgpk,gh->cgkhp', bb, eye)
        return m.reshape(S5_CHUNKS, LANES, gpc * SSM_STATE)

    def readout(cc):
        cc = cc.astype(F32).reshape(S5_CHUNKS, gpc, SSM_GROUP, SSM_STATE)
        m = jnp.einsum('cgkp,gh->cgphk', cc, eye)
        return m.reshape(S5_CHUNKS, gpc * SSM_STATE, LANES)

    bw = jnp.concatenate([drive(bb_r), drive(bb_i)], axis=2).astype(BF16)
    cw = jnp.concatenate([readout(c_re), -readout(c_im)], axis=1).astype(BF16)
    return bw, cw, ab_r.reshape(S5_SLABS, LANES), ab_i.reshape(S5_SLABS, LANES)


def _s5(proj, bw, cw, are, aim, d_skip, bsz, seq, tc):
    nt = seq // tc
    pitch = tc + SUBLANES
    kern = functools.partial(_s5_kernel, tc=tc, pitch=pitch)
    const3 = lambda b, i: (0, 0, 0)
    const2 = lambda b, i: (0, 0)
    return pl.pallas_call(
        kern,
        grid=(bsz, nt),
        in_specs=[
            pl.BlockSpec((tc, D_SSM), lambda b, i: (b * nt + i, COL_U // D_SSM)),
            pl.BlockSpec(bw.shape, const3),
            pl.BlockSpec(cw.shape, const3),
            pl.BlockSpec(are.shape, const2),
            pl.BlockSpec(aim.shape, const2),
            pl.BlockSpec((1, D_SSM), const2),
        ],
        out_specs=pl.BlockSpec((tc, D_SSM), lambda b, i: (b * nt + i, 0)),
        out_shape=jax.ShapeDtypeStruct((bsz * seq, D_SSM), BF16),
        scratch_shapes=[
            pltpu.VMEM((2 * S5_SLABS * pitch, LANES), F32),
            pltpu.VMEM((2 * S5_SLABS, LANES), F32),
        ],
        compiler_params=_cparams(2),
        name="s5",
    )(proj, bw, cw, are, aim, d_skip)


def _split3(x):
    hi = x.astype(BF16)
    r1 = x - hi.astype(F32)
    mid = r1.astype(BF16)
    lo = (r1 - mid.astype(F32)).astype(BF16)
    return hi, mid, lo


def _gla_kernel(q_ref, k_ref, v_ref, r_ref, alr_ref, wg_ref, bg_ref, nw_ref, mt_ref, mo_ref,
                o_ref, cum_ref, last_ref, st_ref, qd_ref, ki_ref, kt_ref, u_ref, sb_ref, *, tc):
    @pl.when(pl.program_id(1) == 0)
    def _():
        st_ref[...] = jnp.zeros_like(st_ref)

    z = jnp.dot(alr_ref[...], wg_ref[...], preferred_element_type=F32) + bg_ref[...]
    log_a = (jnp.minimum(z, 0.0) - jnp.log1p(jnp.exp(-jnp.abs(z)))) * (1.0 / GLA_GATE_NORM)
    parts = _split3(log_a)
    mt = mt_ref[...]
    mo = mo_ref[...]
    cum_ref[...] = sum(jnp.dot(mt, p, preferred_element_type=F32) for p in parts)
    last_ref[...] = sum(jnp.dot(mo, p, preferred_element_type=F32) for p in parts)

    rows_i = lax.broadcasted_iota(jnp.int32, (GLA_CHUNK, GLA_CHUNK), 0)
    cols_i = lax.broadcasted_iota(jnp.int32, (GLA_CHUNK, GLA_CHUNK), 1)
    tri = rows_i >= cols_i
    scale = GLA_DK ** -0.5
    nt_dims = (((1,), (1,)), ((), ()))
    tn_dims = (((0,), (0,)), ((), ()))

    n_chunks = tc // GLA_CHUNK
    cells = [(n, hd) for n in range(n_chunks) for hd in range(GLA_HEADS)]
    rows_of = lambda n: slice(n * GLA_CHUNK, (n + 1) * GLA_CHUNK)
    kc_of = lambda hd: slice(hd * GLA_DK, (hd + 1) * GLA_DK)
    vc_of = lambda hd: slice(hd * GLA_DV, (hd + 1) * GLA_DV)

    cum = cum_ref[...]
    last = last_ref[...]
    kf = k_ref[...].astype(F32)
    qd_ref[...] = (q_ref[...].astype(F32) * scale * jnp.exp(cum)).astype(BF16)
    ki_ref[...] = (kf * jnp.exp(-cum)).astype(BF16)
    kt_ref[...] = (kf * jnp.exp(last - cum)).astype(BF16)

    sc = {}
    for n, hd in cells:
        s = lax.dot_general(qd_ref[rows_of(n), kc_of(hd)], ki_ref[rows_of(n), kc_of(hd)], nt_dims,
                            preferred_element_type=F32)
        sc[n, hd] = jnp.where(tri, s, 0.0).astype(BF16)
    for n, hd in cells:
        u_ref[n, hd] = lax.dot_general(v_ref[rows_of(n), vc_of(hd)], kt_ref[rows_of(n), kc_of(hd)],
                                       tn_dims, preferred_element_type=F32)
    for hd in range(GLA_HEADS):
        st = st_ref[hd]
        for n in range(n_chunks):
            sb_ref[n, hd] = st.astype(BF16)
            dl = jnp.exp(last_ref[n * GLA_CHUNK:n * GLA_CHUNK + 1, kc_of(hd)])
            st = dl * st + u_ref[n, hd]
        st_ref[hd] = st
    for n, hd in cells:
        vv = v_ref[rows_of(n), vc_of(hd)]
        o = jnp.dot(sc[n, hd], vv, preferred_element_type=F32)
        o = o + lax.dot_general(qd_ref[rows_of(n), kc_of(hd)], sb_ref[n, hd], nt_dims,
                                preferred_element_type=F32)
        on = _rms_rows(o, nw_ref[...])
        rr = r_ref[rows_of(n), vc_of(hd)].astype(F32)
        o_ref[rows_of(n), vc_of(hd)] = (on * (rr * jax.nn.sigmoid(rr))).astype(o_ref.dtype)


def _gla(proj, alr, wg, bg, nw, bsz, seq, tc):
    nt = seq // tc
    hk = GLA_HEADS * GLA_DK
    hv = GLA_HEADS * GLA_DV
    pos = np.arange(tc)
    same = (pos[:, None] // GLA_CHUNK) == (pos[None, :] // GLA_CHUNK)
    mt = jnp.asarray(same & (pos[:, None] >= pos[None, :]), BF16)
    mo = jnp.asarray(same, BF16)
    kern = functools.partial(_gla_kernel, tc=tc)
    const2 = lambda b, i: (0, 0)
    row = lambda col: (lambda b, i: (b * nt + i, col))
    return pl.pallas_call(
        kern,
        grid=(bsz, nt),
        in_specs=[
            pl.BlockSpec((tc, hk), row(COL_QG // hk)),
            pl.BlockSpec((tc, hk), row(COL_KG // hk)),
            pl.BlockSpec((tc, hv), row(COL_VG // hv)),
            pl.BlockSpec((tc, hv), row(COL_RG // hv)),
            pl.BlockSpec((tc, LANES), row(0)),
            pl.BlockSpec((LANES, hk), const2),
            pl.BlockSpec((1, hk), const2),
            pl.BlockSpec((1, GLA_DV), const2),
            pl.BlockSpec((tc, tc), const2),
            pl.BlockSpec((tc, tc), const2),
        ],
        out_specs=pl.BlockSpec((tc, hv), row(0)),
        out_shape=jax.ShapeDtypeStruct((bsz * seq, hv), BF16),
        scratch_shapes=[
            pltpu.VMEM((tc, hk), F32),
            pltpu.VMEM((tc, hk), F32),
            pltpu.VMEM((GLA_HEADS, GLA_DV, GLA_DK), F32),
            pltpu.VMEM((tc, hk), BF16),
            pltpu.VMEM((tc, hk), BF16),
            pltpu.VMEM((tc, hk), BF16),
            pltpu.VMEM((tc // GLA_CHUNK, GLA_HEADS, GLA_DV, GLA_DK), F32),
            pltpu.VMEM((tc // GLA_CHUNK, GLA_HEADS, GLA_DV, GLA_DK), BF16),
        ],
        compiler_params=_cparams(2),
        name="gla",
    )(proj, proj, proj, proj, alr, wg, bg, nw, mt, mo)


ATTN_ROW_CHUNK = 16


def _attn_kernel(q_ref, k_ref, v_ref, pos_ref, one_ref, lq1_ref, lk1_ref, lq2_ref, lk2_ref, nw_ref,
                 o_ref, qa_ref, s_ref, p_ref, acc_ref, m_ref, l_ref, al_ref,
                 *, tq, lambda_init):
    head = pl.program_id(1)
    qi = pl.program_id(2)
    slope2 = jnp.float32(ALIBI_SLOPES[-1] * LOG2E)
    for hd in range(DIFF_HEADS - 1):
        slope2 = jnp.where(head == hd, jnp.float32(ALIBI_SLOPES[hd] * LOG2E), slope2)
    nt_dims = (((1,), (1,)), ((), ()))
    n_lane_chunks = tq // LANES

    for m in range(2):
        qa_ref[m] = jnp.concatenate([q_ref[:, m * DIFF_DK:(m + 1) * DIFF_DK], one_ref[...]], axis=1)
    acc_ref[...] = jnp.zeros_like(acc_ref)
    l_ref[...] = jnp.zeros_like(l_ref)
    m_ref[...] = jnp.full(m_ref.shape, -jnp.inf, F32)
    pos = pos_ref[0]

    def tile_rows(j):
        return pl.ds(pl.multiple_of(j * tq, tq), tq)

    def tile_bias(j):
        return -slope2 * ((qi - j) * tq).astype(F32)

    def scores(m, j, masked):
        k_aug = jnp.concatenate([k_ref[tile_rows(j), m * DIFF_DK:(m + 1) * DIFF_DK], pos], axis=1)
        s = lax.dot_general(qa_ref[m], k_aug, nt_dims, preferred_element_type=F32)
        if masked:
            keep = (lax.broadcasted_iota(jnp.int32, (tq, tq), 0)
                    >= lax.broadcasted_iota(jnp.int32, (tq, tq), 1))
            s = jnp.where(keep, s, -jnp.inf)
        s_ref[m] = s
        mx = jnp.broadcast_to(jnp.max(s, axis=-1, keepdims=True), (tq, LANES))
        m_prev = m_ref[m]
        m_new = jnp.maximum(m_prev, mx + tile_bias(j))
        al_ref[m] = jnp.exp2(m_prev - m_new)
        m_ref[m] = m_new

    def accumulate(m, j):
        tb = tile_bias(j)
        for c in range(tq // ATTN_ROW_CHUNK):
            rr = pl.ds(c * ATTN_ROW_CHUNK, ATTN_ROW_CHUNK)
            sh = m_ref[m, rr, :] - tb
            part = None
            for lc in range(n_lane_chunks):
                lanes = slice(lc * LANES, (lc + 1) * LANES)
                p = jnp.exp2(s_ref[m, rr, lanes] - sh)
                p_ref[m, rr, lanes] = p.astype(BF16)
                part = p if part is None else part + p
            l_ref[m, rr, :] = al_ref[m, rr, :] * l_ref[m, rr, :] + part
        pv = jnp.dot(p_ref[m], v_ref[tile_rows(j), :], preferred_element_type=F32)
        al = al_ref[m]
        for hv in range(DIFF_DV // LANES):
            lanes = slice(hv * LANES, (hv + 1) * LANES)
            acc_ref[m, :, lanes] = al * acc_ref[m, :, lanes] + pv[:, lanes]

    def step(j, mask_next):
        accumulate(0, j)
        scores(1, j, False)
        accumulate(1, j)
        scores(0, j + 1, mask_next)

    @pl.when(qi == 0)
    def _():
        scores(0, 0, True)

    @pl.when(qi > 0)
    def _():
        scores(0, 0, False)

    def body(j, carry):
        step(j, False)
        return carry

    lax.fori_loop(0, qi - 1, body, 0)

    @pl.when(qi > 0)
    def _():
        step(qi - 1, True)

    accumulate(0, qi)
    scores(1, qi, True)
    accumulate(1, qi)

    lam = (jnp.exp(jnp.sum(lq1_ref[...] * lk1_ref[...], axis=-1, keepdims=True))
           - jnp.exp(jnp.sum(lq2_ref[...] * lk2_ref[...], axis=-1, keepdims=True)) + lambda_init)
    l1 = jnp.sum(l_ref[0], axis=-1, keepdims=True)
    l2 = jnp.sum(l_ref[1], axis=-1, keepdims=True)
    o = acc_ref[0] / l1 - lam * (acc_ref[1] / l2)
    o = _rms_rows(o, nw_ref[...]) * (1.0 - lambda_init)
    o_ref[...] = o.astype(o_ref.dtype)


def _attn_pos_columns(tq):
    slopes2 = np.asarray([np.float32(s * LOG2E) for s in ALIBI_SLOPES], np.float32)
    rest = np.arange(tq, dtype=np.float32)[None, :] * slopes2[:, None]
    pos = np.zeros((DIFF_HEADS, tq, LANES), BF16)
    one = np.zeros((tq, LANES), BF16)
    for col in range(3):
        part = rest.astype(BF16)
        pos[:, :, col] = part
        one[:, col] = 1.0
        rest = rest - part.astype(np.float32)
    return jnp.asarray(pos), jnp.asarray(one)


def _attn(proj, lq1, lk1, lq2, lk2, nw, bsz, seq, tq, lambda_init):
    nq = seq // tq
    w2 = 2 * DIFF_DK
    kern = functools.partial(_attn_kernel, tq=tq, lambda_init=lambda_init)
    const2 = lambda b, h, i: (0, 0)
    pos, one = _attn_pos_columns(tq)
    stat = pltpu.VMEM((2, tq, LANES), F32)
    return pl.pallas_call(
        kern,
        grid=(bsz, DIFF_HEADS, nq),
        in_specs=[
            pl.BlockSpec((tq, w2), lambda b, h, i: (b * nq + i, COL_QD // w2 + h)),
            pl.BlockSpec((seq, w2), lambda b, h, i: (b, COL_KD // w2 + h),
                         pipeline_mode=pl.Buffered(1)),
            pl.BlockSpec((seq, DIFF_DV), lambda b, h, i: (b, COL_VD // DIFF_DV + h),
                         pipeline_mode=pl.Buffered(1)),
            pl.BlockSpec((1, tq, LANES), lambda b, h, i: (h, 0, 0)),
            pl.BlockSpec((tq, LANES), const2),
            pl.BlockSpec((1, DIFF_DK), const2),
            pl.BlockSpec((1, DIFF_DK), const2),
            pl.BlockSpec((1, DIFF_DK), const2),
            pl.BlockSpec((1, DIFF_DK), const2),
            pl.BlockSpec((1, DIFF_DV), const2),
        ],
        out_specs=pl.BlockSpec((tq, DIFF_DV), lambda b, h, i: (b * nq + i, h)),
        out_shape=jax.ShapeDtypeStruct((bsz * seq, DIFF_HEADS * DIFF_DV), BF16),
        scratch_shapes=[
            pltpu.VMEM((2, tq, 2 * DIFF_DK), BF16),
            pltpu.VMEM((2, tq, tq), F32),
            pltpu.VMEM((2, tq, tq), BF16),
            pltpu.VMEM((2, tq, DIFF_DV), F32),
            stat, stat, stat,
        ],
        compiler_params=_cparams(3),
        name="diffattn",
    )(proj, proj, proj, pos, one, lq1, lk1, lq2, lk2, nw)


def _merge_kernel(gy_ref, ob_ref, oc_ref, g0_ref, g1_ref, g2_ref, wgv_ref, wgg_ref, wb_ref, wc_ref,
                  wo_ref, h_ref, o_ref, acc_ref):
    j = pl.program_id(1)

    @pl.when(j == 0)
    def _():
        acc_ref[...] = jnp.zeros_like(acc_ref)

    gy = gy_ref[...]
    val = jnp.dot(gy, wgv_ref[...], preferred_element_type=F32)
    gate = jnp.dot(gy, wgg_ref[...], preferred_element_type=F32)
    ya = val * jax.nn.sigmoid(gate)
    yb = jnp.dot(ob_ref[...], wb_ref[...], preferred_element_type=F32)
    yc = jnp.dot(oc_ref[...], wc_ref[...], preferred_element_type=F32)
    merged = (jax.nn.sigmoid(g0_ref[...].astype(F32)) * ya
              + jax.nn.sigmoid(g1_ref[...].astype(F32)) * yb
              + jax.nn.sigmoid(g2_ref[...].astype(F32)) * yc)
    acc_ref[...] += jnp.dot(merged.astype(BF16), wo_ref[...], preferred_element_type=F32)

    @pl.when(j == pl.num_programs(1) - 1)
    def _():
        o_ref[...] = h_ref[...] + acc_ref[...]


def _merge(gy, ob, oc, proj, w_glu, wbr_g, wbr_d, w_out, h, tm, tn):
    t = h.shape[0]
    nj = D_MODEL // tn
    gt0 = COL_GT // tn
    left = lambda i, j: (i, 0)
    return pl.pallas_call(
        _merge_kernel,
        grid=(t // tm, nj),
        in_specs=[
            pl.BlockSpec((tm, D_SSM), left),
            pl.BlockSpec((tm, GLA_HEADS * GLA_DV), left),
            pl.BlockSpec((tm, DIFF_HEADS * DIFF_DV), left),
            pl.BlockSpec((tm, tn), lambda i, j: (i, gt0 + j)),
            pl.BlockSpec((tm, tn), lambda i, j: (i, gt0 + nj + j)),
            pl.BlockSpec((tm, tn), lambda i, j: (i, gt0 + 2 * nj + j)),
            pl.BlockSpec((D_SSM, tn), lambda i, j: (0, j)),
            pl.BlockSpec((D_SSM, tn), lambda i, j: (0, nj + j)),
            pl.BlockSpec((GLA_HEADS * GLA_DV, tn), lambda i, j: (0, j)),
            pl.BlockSpec((DIFF_HEADS * DIFF_DV, tn), lambda i, j: (0, j)),
            pl.BlockSpec((tn, D_MODEL), lambda i, j: (j, 0)),
            pl.BlockSpec((tm, D_MODEL), left),
        ],
        out_specs=pl.BlockSpec((tm, D_MODEL), left),
        out_shape=jax.ShapeDtypeStruct((t, D_MODEL), F32),
        scratch_shapes=[pltpu.VMEM((tm, D_MODEL), F32)],
        compiler_params=_cparams(2),
        name="merge",
    )(gy, ob, oc, proj, proj, proj, w_glu, w_glu, wbr_g, wbr_d, w_out, h)


def _ffn_kernel(h_ref, halo_ref, nw_ref, wv_ref, wg_ref, cwv_ref, cwg_ref, cbv_ref, cbg_ref, wd_ref,
                fnw_ref, o_ref, hn_ref, acc_ref, *, tm, tiles_per_seq, final_norm):
    i = pl.program_id(0)
    j = pl.program_id(1)

    @pl.when(j == 0)
    def _():
        keep = (i % tiles_per_seq != 0).astype(F32)
        hn_ref[0:HALO, :] = (_rms_rows(halo_ref[...], nw_ref[...]) * keep).astype(BF16)
        hn_ref[HALO:, :] = _rms_rows(h_ref[...], nw_ref[...]).astype(BF16)
        acc_ref[...] = jnp.zeros_like(acc_ref)

    hn = hn_ref[...]

    def conv(w_ref, cw_ref, cb_ref):
        u = jnp.dot(hn, w_ref[...], preferred_element_type=F32)
        acc = cb_ref[...]
        for tap in range(CONV_W):
            lo = HALO - (CONV_W - 1) + tap
            acc = acc + u[lo:lo + tm, :] * cw_ref[tap:tap + 1, :]
        return acc

    val = conv(wv_ref, cwv_ref, cbv_ref)
    gate = conv(wg_ref, cwg_ref, cbg_ref)
    act = (gate * jax.nn.sigmoid(gate) * val).astype(BF16)
    acc_ref[...] += jnp.dot(act, wd_ref[...], preferred_element_type=F32)

    @pl.when(j == pl.num_programs(1) - 1)
    def _():
        out = h_ref[...] + acc_ref[...]
        if final_norm:
            out = _rms_rows(out, fnw_ref[...])
        o_ref[...] = out


def _ffn(h, nw, wv, wg, cwv, cwg, cbv, cbg, wd, fnw, seq, tm, final_norm):
    t = h.shape[0]
    tf = FFN_TILE
    kern = functools.partial(_ffn_kernel, tm=tm, tiles_per_seq=seq // tm, final_norm=final_norm)
    halo_blocks = tm // HALO
    const2 = lambda i, j: (0, 0)
    col = lambda i, j: (0, j)
    return pl.pallas_call(
        kern,
        grid=(t // tm, D_FF_PAD // tf),
        in_specs=[
            pl.BlockSpec((tm, D_MODEL), lambda i, j: (i, 0)),
            pl.BlockSpec((HALO, D_MODEL), lambda i, j: (jnp.maximum(i * halo_blocks - 1, 0), 0)),
            pl.BlockSpec((1, D_MODEL), const2),
            pl.BlockSpec((D_MODEL, tf), col),
            pl.BlockSpec((D_MODEL, tf), col),
            pl.BlockSpec((CONV_W, tf), col),
            pl.BlockSpec((CONV_W, tf), col),
            pl.BlockSpec((1, tf), col),
            pl.BlockSpec((1, tf), col),
            pl.BlockSpec((tf, D_MODEL), lambda i, j: (j, 0)),
            pl.BlockSpec((1, D_MODEL), const2),
        ],
        out_specs=pl.BlockSpec((tm, D_MODEL), lambda i, j: (i, 0)),
        out_shape=jax.ShapeDtypeStruct((t, D_MODEL), F32),
        scratch_shapes=[
            pltpu.VMEM((HALO + tm, D_MODEL), BF16),
            pltpu.VMEM((tm, D_MODEL), F32),
        ],
        compiler_params=_cparams(2),
        name="ffn",
    )(h, h, nw, wv, wg, cwv, cwg, cbv, cbg, wd, fnw)


def _pad_cols(a, n):
    return jnp.pad(a, ((0, 0), (0, n - a.shape[1])))


def _tile(n, pref):
    return pref if n % pref == 0 else n


def kernel(x, norm_mix_w, w_in, ssm_a_re, ssm_a_im, ssm_log_dt, ssm_b_re, ssm_b_im, ssm_c_re, ssm_c_im, ssm_d, ssm_w_glu, gla_w_gate, gla_b_gate, gla_norm_w, gla_w_br, diff_lam_q1, diff_lam_k1, diff_lam_q2, diff_lam_k2, diff_norm_w, diff_w_br, w_out, norm_ffn_w, ffn_w_up, ffn_conv_w, ffn_conv_b, ffn_w_down, norm_final_w):
    bsz, seq, _ = x.shape
    t = bsz * seq
    h = x.reshape(t, D_MODEL).astype(F32)
    row = lambda v: v.reshape(1, -1).astype(F32)

    for l in range(DEPTH):
        lambda_init = 0.8 - 0.6 * math.exp(-0.3 * l)
        w = w_in[l]
        w_main = jnp.concatenate([w[:, :ALR_ORIG], w[:, ALR_ORIG + GLA_GATE_RANK:]], axis=1).astype(BF16)
        w_alr = _pad_cols(w[:, ALR_ORIG:ALR_ORIG + GLA_GATE_RANK], LANES).astype(BF16)
        proj, alr = _inproj(h, row(norm_mix_w[l]), w_main, w_alr, _tile(t, 1024), 1024)

        bw, cw, are, aim = _s5_weights(ssm_a_re[l], ssm_a_im[l], ssm_log_dt[l], ssm_b_re[l],
                                       ssm_b_im[l], ssm_c_re[l], ssm_c_im[l])
        gy = _s5(proj, bw, cw, are, aim, row(ssm_d[l]), bsz, seq, _tile(seq, 256))

        wg = jnp.pad(gla_w_gate[l], ((0, LANES - GLA_GATE_RANK), (0, 0))).astype(BF16)
        ob = _gla(proj, alr, wg, row(gla_b_gate[l]), row(gla_norm_w[l]), bsz, seq, _tile(seq, 256))

        oc = _attn(proj, row(diff_lam_q1[l]), row(diff_lam_k1[l]), row(diff_lam_q2[l]),
                   row(diff_lam_k2[l]), row(diff_norm_w[l]), bsz, seq, _tile(seq, 1024), lambda_init)

        h = _merge(gy, ob, oc, proj, ssm_w_glu[l].astype(BF16), gla_w_br[l].astype(BF16),
                   diff_w_br[l].astype(BF16), w_out[l].astype(BF16), h, _tile(t, 512), 512)

        wup = ffn_w_up[l]
        cwf = ffn_conv_w[l].astype(F32)
        cbf = ffn_conv_b[l].reshape(1, -1).astype(F32)
        h = _ffn(h, row(norm_ffn_w[l]),
                 _pad_cols(wup[:, :D_FF], D_FF_PAD).astype(BF16),
                 _pad_cols(wup[:, D_FF:], D_FF_PAD).astype(BF16),
                 _pad_cols(cwf[:, :D_FF], D_FF_PAD), _pad_cols(cwf[:, D_FF:], D_FF_PAD),
                 _pad_cols(cbf[:, :D_FF], D_FF_PAD), _pad_cols(cbf[:, D_FF:], D_FF_PAD),
                 jnp.pad(ffn_w_down[l], ((0, D_FF_PAD - D_FF), (0, 0))).astype(BF16),
                 row(norm_final_w), seq, _tile(seq, 512), l == DEPTH - 1)

    return h.reshape(bsz, seq, D_MODEL).astype(x.dtype)
```

```python
import functools
import math

import jax
import jax.numpy as jnp
import numpy as np
from jax import lax
from jax.experimental import pallas as pl
from jax.experimental.pallas import tpu as pltpu

F32 = jnp.float32
BF16 = jnp.bfloat16

D_MODEL = 2048
DEPTH = 2
D_SSM = 1024
SSM_GROUP = 16
N_SSM_GROUPS = D_SSM // SSM_GROUP
SSM_STATE = 64
GLA_HEADS = 4
GLA_DK = 128
GLA_DV = 256
GLA_GATE_RANK = 16
GLA_GATE_NORM = 16.0
GLA_CHUNK = 32
DIFF_HEADS = 4
DIFF_DK = 128
DIFF_DV = 256
N_BRANCH = 3
D_FF = 5504
CONV_W = 3
EPS = 1e-6
ALIBI_SLOPES = tuple(2.0 ** (-8.0 * (h + 1) / DIFF_HEADS) for h in range(DIFF_HEADS))
LOG2E = math.log2(math.e)

LANES = 128
SUBLANES = 8
BF16_ROWS = 16

COL_U = 0
COL_QG = COL_U + D_SSM
COL_KG = COL_QG + GLA_HEADS * GLA_DK
COL_VG = COL_KG + GLA_HEADS * GLA_DK
COL_RG = COL_VG + GLA_HEADS * GLA_DV
COL_QD = COL_RG + GLA_HEADS * GLA_DV
COL_KD = COL_QD + DIFF_HEADS * 2 * DIFF_DK
COL_VD = COL_KD + DIFF_HEADS * 2 * DIFF_DK
COL_GT = COL_VD + DIFF_HEADS * DIFF_DV
N_MAIN = COL_GT + N_BRANCH * D_MODEL
ALR_ORIG = COL_QD

D_FF_PAD = 5632
FFN_TILE = 512
HALO = BF16_ROWS

VMEM_LIMIT = 56 * 1024 * 1024


def _cparams(n_axes, flags=None):
    return pltpu.CompilerParams(
        dimension_semantics=("arbitrary",) * n_axes, vmem_limit_bytes=VMEM_LIMIT, flags=flags)


def _rms_rows(x, w):
    ms = jnp.mean(x * x, axis=-1, keepdims=True)
    return x * lax.rsqrt(ms + EPS) * w


def _inproj_kernel(h_ref, nw_ref, w_ref, walr_ref, cs_ref, o_ref, alr_ref, hn_ref):
    @pl.when(pl.program_id(1) == 0)
    def _():
        hn = _rms_rows(h_ref[...], nw_ref[...]).astype(BF16)
        hn_ref[...] = hn
        alr_ref[...] = jnp.dot(hn, walr_ref[...], preferred_element_type=F32).astype(alr_ref.dtype)

    acc = jnp.dot(hn_ref[...], w_ref[...], preferred_element_type=F32)
    o_ref[...] = (acc * cs_ref[...]).astype(o_ref.dtype)


def _inproj(h, nw, w_main, w_alr, tm, tn):
    t = h.shape[0]
    cs = jnp.ones((1, N_MAIN), F32).at[:, COL_QD:COL_KD].set(DIFF_DK ** -0.5 * LOG2E)
    return pl.pallas_call(
        _inproj_kernel,
        grid=(t // tm, N_MAIN // tn),
        in_specs=[
            pl.BlockSpec((tm, D_MODEL), lambda i, j: (i, 0)),
            pl.BlockSpec((1, D_MODEL), lambda i, j: (0, 0)),
            pl.BlockSpec((D_MODEL, tn), lambda i, j: (0, j)),
            pl.BlockSpec((D_MODEL, LANES), lambda i, j: (0, 0)),
            pl.BlockSpec((1, tn), lambda i, j: (0, j)),
        ],
        out_specs=[
            pl.BlockSpec((tm, tn), lambda i, j: (i, j)),
            pl.BlockSpec((tm, LANES), lambda i, j: (i, 0)),
        ],
        out_shape=[
            jax.ShapeDtypeStruct((t, N_MAIN), BF16),
            jax.ShapeDtypeStruct((t, LANES), BF16),
        ],
        scratch_shapes=[pltpu.VMEM((tm, D_MODEL), BF16)],
        compiler_params=_cparams(2),
        name="inproj",
    )(h, nw, w_main, w_alr, cs)


S5_CHUNKS = D_SSM // LANES
S5_SLABS = N_SSM_GROUPS * SSM_STATE // LANES
S5_SLABS_PER_CHUNK = S5_SLABS // S5_CHUNKS
S5_VREGS = S5_SLABS // SUBLANES


def _s5_kernel(u_ref, bw_ref, cw_ref, are_ref, aim_ref, d_ref, o_ref, xs_ref, st_ref, *, nb, tc, pitch):
    @pl.when(pl.program_id(0) == 0)
    def _():
        st_ref[...] = jnp.zeros_like(st_ref)

    half = S5_SLABS_PER_CHUNK * LANES
    a_re = [are_ref[k * SUBLANES:(k + 1) * SUBLANES, :] for k in range(S5_VREGS)]
    a_im = [aim_ref[k * SUBLANES:(k + 1) * SUBLANES, :] for k in range(S5_VREGS)]

    def slab(b, r):
        return (b * 2 * S5_SLABS + r) * pitch

    def drive(b):
        for c in range(S5_CHUNKS):
            x = jnp.dot(u_ref[b, :, c * LANES:(c + 1) * LANES], bw_ref[c], preferred_element_type=F32)
            for q in range(S5_SLABS_PER_CHUNK):
                r = c * S5_SLABS_PER_CHUNK + q
                xs_ref[pl.ds(slab(b, r), tc), :] = x[:, q * LANES:(q + 1) * LANES]
                xs_ref[pl.ds(slab(b, S5_SLABS + r), tc), :] = x[:, half + q * LANES:half + (q + 1) * LANES]

    def scan(b):
        st0 = b * 2 * S5_SLABS
        state = []
        for k in range(S5_VREGS):
            state.append([st_ref[st0 + k * SUBLANES:st0 + (k + 1) * SUBLANES, :],
                          st_ref[st0 + S5_SLABS + k * SUBLANES:st0 + S5_SLABS + (k + 1) * SUBLANES, :]])
        for t in range(tc):
            for k in range(S5_VREGS):
                re, im = state[k]
                idx_re = pl.ds(slab(b, k * SUBLANES) + t, SUBLANES, stride=pitch)
                idx_im = pl.ds(slab(b, S5_SLABS + k * SUBLANES) + t, SUBLANES, stride=pitch)
                nre = a_re[k] * re - a_im[k] * im + xs_ref[idx_re, :]
                nim = a_re[k] * im + a_im[k] * re + xs_ref[idx_im, :]
                xs_ref[idx_re, :] = nre
                xs_ref[idx_im, :] = nim
                state[k] = [nre, nim]
        for k in range(S5_VREGS):
            st_ref[st0 + k * SUBLANES:st0 + (k + 1) * SUBLANES, :] = state[k][0]
            st_ref[st0 + S5_SLABS + k * SUBLANES:st0 + S5_SLABS + (k + 1) * SUBLANES, :] = state[k][1]

    def readout(b):
        for c in range(S5_CHUNKS):
            parts = []
            for base in (0, S5_SLABS):
                for q in range(S5_SLABS_PER_CHUNK):
                    r = base + c * S5_SLABS_PER_CHUNK + q
                    parts.append(xs_ref[pl.ds(slab(b, r), tc), :].astype(BF16))
            hcat = jnp.concatenate(parts, axis=1)
            y = jnp.dot(hcat, cw_ref[c], preferred_element_type=F32)
            lanes = slice(c * LANES, (c + 1) * LANES)
            y = y + d_ref[:, lanes] * u_ref[b, :, lanes].astype(F32)
            o_ref[b, :, lanes] = jax.nn.gelu(y).astype(o_ref.dtype)

    for b in range(nb):
        drive(b)
    for b in range(nb):
        scan(b)
        readout(b)


def _s5_weights(a_re, a_im, log_dt, b_re, b_im, c_re, c_im):
    lam_r = jnp.minimum(a_re.astype(F32), -1e-4)
    lam_i = a_im.astype(F32)
    dt = jnp.exp(log_dt.astype(F32))[:, None]
    mag = jnp.exp(dt * lam_r)
    ab_r = mag * jnp.cos(dt * lam_i)
    ab_i = mag * jnp.sin(dt * lam_i)
    den = lam_r * lam_r + lam_i * lam_i
    f_r = ((ab_r - 1.0) * lam_r + ab_i * lam_i) / den
    f_i = (ab_i * lam_r - (ab_r - 1.0) * lam_i) / den
    br = b_re.astype(F32)
    bi = b_im.astype(F32)
    bb_r = f_r[..., None] * br - f_i[..., None] * bi
    bb_i = f_r[..., None] * bi + f_i[..., None] * br
    gpc = LANES // SSM_GROUP
    eye = jnp.eye(gpc, dtype=F32)

    def drive(bb):
        bb = bb.reshape(S5_CHUNKS, gpc, SSM_STATE, SSM_GROUP)
        m = jnp.einsum('cgpk,gh->cgkhp', bb, eye)
        return m.reshape(S5_CHUNKS, LANES, gpc * SSM_STATE)

    def readout(cc):
        cc = cc.astype(F32).reshape(S5_CHUNKS, gpc, SSM_GROUP, SSM_STATE)
        m = jnp.einsum('cgkp,gh->cgphk', cc, eye)
        return m.reshape(S5_CHUNKS, gpc * SSM_STATE, LANES)

    bw = jnp.concatenate([drive(bb_r), drive(bb_i)], axis=2).astype(BF16)
    cw = jnp.concatenate([readout(c_re), -readout(c_im)], axis=1).astype(BF16)
    return bw, cw, ab_r.reshape(S5_SLABS, LANES), ab_i.reshape(S5_SLABS, LANES)


def _s5(proj, bw, cw, are, aim, d_skip, bsz, seq, tc):
    nt = seq // tc
    pitch = tc + SUBLANES
    kern = functools.partial(_s5_kernel, nb=bsz, tc=tc, pitch=pitch)
    const3 = lambda i: (0, 0, 0)
    const2 = lambda i: (0, 0)
    out = pl.pallas_call(
        kern,
        grid=(nt,),
        in_specs=[
            pl.BlockSpec((bsz, tc, D_SSM), lambda i: (0, i, COL_U // D_SSM)),
            pl.BlockSpec(bw.shape, const3),
            pl.BlockSpec(cw.shape, const3),
            pl.BlockSpec(are.shape, const2),
            pl.BlockSpec(aim.shape, const2),
            pl.BlockSpec((1, D_SSM), const2),
        ],
        out_specs=pl.BlockSpec((bsz, tc, D_SSM), lambda i: (0, i, 0)),
        out_shape=jax.ShapeDtypeStruct((bsz, seq, D_SSM), BF16),
        scratch_shapes=[
            pltpu.VMEM((bsz * 2 * S5_SLABS * pitch, LANES), F32),
            pltpu.VMEM((bsz * 2 * S5_SLABS, LANES), F32),
        ],
        compiler_params=_cparams(1),
        name="s5",
    )(proj.reshape(bsz, seq, N_MAIN), bw, cw, are, aim, d_skip)
    return out.reshape(bsz * seq, D_SSM)


def _split3(x):
    hi = x.astype(BF16)
    r1 = x - hi.astype(F32)
    mid = r1.astype(BF16)
    lo = (r1 - mid.astype(F32)).astype(BF16)
    return hi, mid, lo


def _gla_kernel(q_ref, k_ref, v_ref, r_ref, alr_ref, wg_ref, bg_ref, nw_ref, mt_ref, mo_ref,
                o_ref, cum_ref, last_ref, st_ref, qd_ref, ki_ref, kt_ref, u_ref, sb_ref, *, tc):
    @pl.when(pl.program_id(1) == 0)
    def _():
        st_ref[...] = jnp.zeros_like(st_ref)

    z = jnp.dot(alr_ref[...], wg_ref[...], preferred_element_type=F32) + bg_ref[...]
    log_a = (jnp.minimum(z, 0.0) - jnp.log1p(jnp.exp(-jnp.abs(z)))) * (1.0 / GLA_GATE_NORM)
    parts = _split3(log_a)
    mt = mt_ref[...]
    mo = mo_ref[...]
    cum_ref[...] = sum(jnp.dot(mt, p, preferred_element_type=F32) for p in parts)
    last_ref[...] = sum(jnp.dot(mo, p, preferred_element_type=F32) for p in parts)

    rows_i = lax.broadcasted_iota(jnp.int32, (GLA_CHUNK, GLA_CHUNK), 0)
    cols_i = lax.broadcasted_iota(jnp.int32, (GLA_CHUNK, GLA_CHUNK), 1)
    tri = rows_i >= cols_i
    scale = GLA_DK ** -0.5
    nt_dims = (((1,), (1,)), ((), ()))
    tn_dims = (((0,), (0,)), ((), ()))

    n_chunks = tc // GLA_CHUNK
    cells = [(n, hd) for n in range(n_chunks) for hd in range(GLA_HEADS)]
    rows_of = lambda n: slice(n * GLA_CHUNK, (n + 1) * GLA_CHUNK)
    kc_of = lambda hd: slice(hd * GLA_DK, (hd + 1) * GLA_DK)
    vc_of = lambda hd: slice(hd * GLA_DV, (hd + 1) * GLA_DV)

    cum = cum_ref[...]
    last = last_ref[...]
    kf = k_ref[...].astype(F32)
    qd_ref[...] = (q_ref[...].astype(F32) * scale * jnp.exp(cum)).astype(BF16)
    ki_ref[...] = (kf * jnp.exp(-cum)).astype(BF16)
    kt_ref[...] = (kf * jnp.exp(last - cum)).astype(BF16)

    sc = {}
    for n, hd in cells:
        s = lax.dot_general(qd_ref[rows_of(n), kc_of(hd)], ki_ref[rows_of(n), kc_of(hd)], nt_dims,
                            preferred_element_type=F32)
        sc[n, hd] = jnp.where(tri, s, 0.0).astype(BF16)
    for n, hd in cells:
        u_ref[n, hd] = lax.dot_general(v_ref[rows_of(n), vc_of(hd)], kt_ref[rows_of(n), kc_of(hd)],
                                       tn_dims, preferred_element_type=F32)
    for hd in range(GLA_HEADS):
        st = st_ref[hd]
        for n in range(n_chunks):
            sb_ref[n, hd] = st.astype(BF16)
            dl = jnp.exp(last_ref[n * GLA_CHUNK:n * GLA_CHUNK + 1, kc_of(hd)])
            st = dl * st + u_ref[n, hd]
        st_ref[hd] = st
    for n, hd in cells:
        vv = v_ref[rows_of(n), vc_of(hd)]
        o = jnp.dot(sc[n, hd], vv, preferred_element_type=F32)
        o = o + lax.dot_general(qd_ref[rows_of(n), kc_of(hd)], sb_ref[n, hd], nt_dims,
                                preferred_element_type=F32)
        on = _rms_rows(o, nw_ref[...])
        rr = r_ref[rows_of(n), vc_of(hd)].astype(F32)
        o_ref[rows_of(n), vc_of(hd)] = (on * (rr * jax.nn.sigmoid(rr))).astype(o_ref.dtype)


def _gla(proj, alr, wg, bg, nw, bsz, seq, tc):
    nt = seq // tc
    hk = GLA_HEADS * GLA_DK
    hv = GLA_HEADS * GLA_DV
    pos = np.arange(tc)
    same = (pos[:, None] // GLA_CHUNK) == (pos[None, :] // GLA_CHUNK)
    mt = jnp.asarray(same & (pos[:, None] >= pos[None, :]), BF16)
    mo = jnp.asarray(same, BF16)
    kern = functools.partial(_gla_kernel, tc=tc)
    const2 = lambda b, i: (0, 0)
    row = lambda col: (lambda b, i: (b * nt + i, col))
    return pl.pallas_call(
        kern,
        grid=(bsz, nt),
        in_specs=[
            pl.BlockSpec((tc, hk), row(COL_QG // hk)),
            pl.BlockSpec((tc, hk), row(COL_KG // hk)),
            pl.BlockSpec((tc, hv), row(COL_VG // hv)),
            pl.BlockSpec((tc, hv), row(COL_RG // hv)),
            pl.BlockSpec((tc, LANES), row(0)),
            pl.BlockSpec((LANES, hk), const2),
            pl.BlockSpec((1, hk), const2),
            pl.BlockSpec((1, GLA_DV), const2),
            pl.BlockSpec((tc, tc), const2),
            pl.BlockSpec((tc, tc), const2),
        ],
        out_specs=pl.BlockSpec((tc, hv), row(0)),
        out_shape=jax.ShapeDtypeStruct((bsz * seq, hv), BF16),
        scratch_shapes=[
            pltpu.VMEM((tc, hk), F32),
            pltpu.VMEM((tc, hk), F32),
            pltpu.VMEM((GLA_HEADS, GLA_DV, GLA_DK), F32),
            pltpu.VMEM((tc, hk), BF16),
            pltpu.VMEM((tc, hk), BF16),
            pltpu.VMEM((tc, hk), BF16),
            pltpu.VMEM((tc // GLA_CHUNK, GLA_HEADS, GLA_DV, GLA_DK), F32),
            pltpu.VMEM((tc // GLA_CHUNK, GLA_HEADS, GLA_DV, GLA_DK), BF16),
        ],
        compiler_params=_cparams(2),
        name="gla",
    )(proj, proj, proj, proj, alr, wg, bg, nw, mt, mo)


ATTN_ROW_CHUNK = 16


def _attn_kernel(q_ref, k_ref, v_ref, pos_ref, one_ref, lq1_ref, lk1_ref, lq2_ref, lk2_ref, nw_ref,
                 o_ref, qa_ref, s_ref, p_ref, acc_ref, m_ref, l_ref, al_ref,
                 *, tq, lambda_init):
    head = pl.program_id(1)
    qi = pl.program_id(2)
    slope2 = jnp.float32(ALIBI_SLOPES[-1] * LOG2E)
    for hd in range(DIFF_HEADS - 1):
        slope2 = jnp.where(head == hd, jnp.float32(ALIBI_SLOPES[hd] * LOG2E), slope2)
    nt_dims = (((1,), (1,)), ((), ()))
    n_lane_chunks = tq // LANES

    for m in range(2):
        qa_ref[m] = jnp.concatenate([q_ref[:, m * DIFF_DK:(m + 1) * DIFF_DK], one_ref[...]], axis=1)
    acc_ref[...] = jnp.zeros_like(acc_ref)
    l_ref[...] = jnp.zeros_like(l_ref)
    m_ref[...] = jnp.full(m_ref.shape, -jnp.inf, F32)
    pos = pos_ref[0]

    def tile_rows(j):
        return pl.ds(pl.multiple_of(j * tq, tq), tq)

    def tile_bias(j):
        return -slope2 * ((qi - j) * tq).astype(F32)

    def scores(m, j, masked):
        k_aug = jnp.concatenate([k_ref[tile_rows(j), m * DIFF_DK:(m + 1) * DIFF_DK], pos], axis=1)
        s = lax.dot_general(qa_ref[m], k_aug, nt_dims, preferred_element_type=F32)
        if masked:
            keep = (lax.broadcasted_iota(jnp.int32, (tq, tq), 0)
                    >= lax.broadcasted_iota(jnp.int32, (tq, tq), 1))
            s = jnp.where(keep, s, -jnp.inf)
        s_ref[m] = s
        mx = jnp.broadcast_to(jnp.max(s, axis=-1, keepdims=True), (tq, LANES))
        m_prev = m_ref[m]
        m_new = jnp.maximum(m_prev, mx + tile_bias(j))
        al_ref[m] = jnp.exp2(m_prev - m_new)
        m_ref[m] = m_new

    def accumulate(m, j):
        tb = tile_bias(j)
        for c in range(tq // ATTN_ROW_CHUNK):
            rr = pl.ds(c * ATTN_ROW_CHUNK, ATTN_ROW_CHUNK)
            sh = m_ref[m, rr, :] - tb
            for lc in range(n_lane_chunks):
                lanes = slice(lc * LANES, (lc + 1) * LANES)
                p_ref[m, rr, lanes] = jnp.exp2(s_ref[m, rr, lanes] - sh).astype(BF16)
        for c in range(tq // ATTN_ROW_CHUNK):
            rr = pl.ds(c * ATTN_ROW_CHUNK, ATTN_ROW_CHUNK)
            part = None
            for lc in range(n_lane_chunks):
                x = p_ref[m, rr, lc * LANES:(lc + 1) * LANES].astype(F32)
                part = x if part is None else part + x
            l_ref[m, rr, :] = al_ref[m, rr, :] * l_ref[m, rr, :] + part
        pv = jnp.dot(p_ref[m], v_ref[tile_rows(j), :], preferred_element_type=F32)
        al = al_ref[m]
        for hv in range(DIFF_DV // LANES):
            lanes = slice(hv * LANES, (hv + 1) * LANES)
            acc_ref[m, :, lanes] = al * acc_ref[m, :, lanes] + pv[:, lanes]

    def step(j, mask_next):
        accumulate(0, j)
        scores(1, j, False)
        accumulate(1, j)
        scores(0, j + 1, mask_next)

    @pl.when(qi == 0)
    def _():
        scores(0, 0, True)

    @pl.when(qi > 0)
    def _():
        scores(0, 0, False)

    def body(j, carry):
        step(j, False)
        return carry

    lax.fori_loop(0, qi - 1, body, 0)

    @pl.when(qi > 0)
    def _():
        step(qi - 1, True)

    accumulate(0, qi)
    scores(1, qi, True)
    accumulate(1, qi)

    lam = (jnp.exp(jnp.sum(lq1_ref[...] * lk1_ref[...], axis=-1, keepdims=True))
           - jnp.exp(jnp.sum(lq2_ref[...] * lk2_ref[...], axis=-1, keepdims=True)) + lambda_init)
    l1 = jnp.sum(l_ref[0], axis=-1, keepdims=True)
    l2 = jnp.sum(l_ref[1], axis=-1, keepdims=True)
    o = acc_ref[0] / l1 - lam * (acc_ref[1] / l2)
    o = _rms_rows(o, nw_ref[...]) * (1.0 - lambda_init)
    o_ref[...] = o.astype(o_ref.dtype)


def _attn_pos_columns(tq):
    slopes2 = np.asarray([np.float32(s * LOG2E) for s in ALIBI_SLOPES], np.float32)
    rest = np.arange(tq, dtype=np.float32)[None, :] * slopes2[:, None]
    pos = np.zeros((DIFF_HEADS, tq, LANES), BF16)
    one = np.zeros((tq, LANES), BF16)
    for col in range(3):
        part = rest.astype(BF16)
        pos[:, :, col] = part
        one[:, col] = 1.0
        rest = rest - part.astype(np.float32)
    return jnp.asarray(pos), jnp.asarray(one)


def _attn(proj, lq1, lk1, lq2, lk2, nw, bsz, seq, tq, lambda_init):
    nq = seq // tq
    w2 = 2 * DIFF_DK
    kern = functools.partial(_attn_kernel, tq=tq, lambda_init=lambda_init)
    const2 = lambda b, h, i: (0, 0)
    pos, one = _attn_pos_columns(tq)
    stat = pltpu.VMEM((2, tq, LANES), F32)
    return pl.pallas_call(
        kern,
        grid=(bsz, DIFF_HEADS, nq),
        in_specs=[
            pl.BlockSpec((tq, w2), lambda b, h, i: (b * nq + i, COL_QD // w2 + h)),
            pl.BlockSpec((seq, w2), lambda b, h, i: (b, COL_KD // w2 + h),
                         pipeline_mode=pl.Buffered(1)),
            pl.BlockSpec((seq, DIFF_DV), lambda b, h, i: (b, COL_VD // DIFF_DV + h),
                         pipeline_mode=pl.Buffered(1)),
            pl.BlockSpec((1, tq, LANES), lambda b, h, i: (h, 0, 0)),
            pl.BlockSpec((tq, LANES), const2),
            pl.BlockSpec((1, DIFF_DK), const2),
            pl.BlockSpec((1, DIFF_DK), const2),
            pl.BlockSpec((1, DIFF_DK), const2),
            pl.BlockSpec((1, DIFF_DK), const2),
            pl.BlockSpec((1, DIFF_DV), const2),
        ],
        out_specs=pl.BlockSpec((tq, DIFF_DV), lambda b, h, i: (b * nq + i, h)),
        out_shape=jax.ShapeDtypeStruct((bsz * seq, DIFF_HEADS * DIFF_DV), BF16),
        scratch_shapes=[
            pltpu.VMEM((2, tq, 2 * DIFF_DK), BF16),
            pltpu.VMEM((2, tq, tq), F32),
            pltpu.VMEM((2, tq, tq), BF16),
            pltpu.VMEM((2, tq, DIFF_DV), F32),
            stat, stat, stat,
        ],
        compiler_params=_cparams(3),
        name="diffattn",
    )(proj, proj, proj, pos, one, lq1, lk1, lq2, lk2, nw)


def _merge_kernel(gy_ref, ob_ref, oc_ref, g0_ref, g1_ref, g2_ref, wgv_ref, wgg_ref, wb_ref, wc_ref,
                  wo_ref, h_ref, o_ref):
    j = pl.program_id(1)

    @pl.when(j == 0)
    def _():
        o_ref[...] = h_ref[...]

    gy = gy_ref[...]
    val = jnp.dot(gy, wgv_ref[...], preferred_element_type=F32)
    gate = jnp.dot(gy, wgg_ref[...], preferred_element_type=F32)
    ya = val * jax.nn.sigmoid(gate)
    yb = jnp.dot(ob_ref[...], wb_ref[...], preferred_element_type=F32)
    yc = jnp.dot(oc_ref[...], wc_ref[...], preferred_element_type=F32)
    merged = (jax.nn.sigmoid(g0_ref[...].astype(F32)) * ya
              + jax.nn.sigmoid(g1_ref[...].astype(F32)) * yb
              + jax.nn.sigmoid(g2_ref[...].astype(F32)) * yc)
    o_ref[...] += jnp.dot(merged.astype(BF16), wo_ref[...], preferred_element_type=F32)


def _merge(gy, ob, oc, proj, w_glu, wbr_g, wbr_d, w_out, h, tm, tn):
    t = h.shape[0]
    nj = D_MODEL // tn
    gt0 = COL_GT // tn
    left = lambda i, j: (i, 0)
    return pl.pallas_call(
        _merge_kernel,
        grid=(t // tm, nj),
        in_specs=[
            pl.BlockSpec((tm, D_SSM), left),
            pl.BlockSpec((tm, GLA_HEADS * GLA_DV), left),
            pl.BlockSpec((tm, DIFF_HEADS * DIFF_DV), left),
            pl.BlockSpec((tm, tn), lambda i, j: (i, gt0 + j)),
            pl.BlockSpec((tm, tn), lambda i, j: (i, gt0 + nj + j)),
            pl.BlockSpec((tm, tn), lambda i, j: (i, gt0 + 2 * nj + j)),
            pl.BlockSpec((D_SSM, tn), lambda i, j: (0, j)),
            pl.BlockSpec((D_SSM, tn), lambda i, j: (0, nj + j)),
            pl.BlockSpec((GLA_HEADS * GLA_DV, tn), lambda i, j: (0, j)),
            pl.BlockSpec((DIFF_HEADS * DIFF_DV, tn), lambda i, j: (0, j)),
            pl.BlockSpec((tn, D_MODEL), lambda i, j: (j, 0)),
            pl.BlockSpec((tm, D_MODEL), left),
        ],
        out_specs=pl.BlockSpec((tm, D_MODEL), left),
        out_shape=jax.ShapeDtypeStruct((t, D_MODEL), F32),
        compiler_params=_cparams(2),
        name="merge",
    )(gy, ob, oc, proj, proj, proj, w_glu, w_glu, wbr_g, wbr_d, w_out, h)


def _ffn_kernel(h_ref, halo_ref, nw_ref, wv_ref, wg_ref, cwv_ref, cwg_ref, cbv_ref, cbg_ref, wd_ref,
                fnw_ref, o_ref, hn_ref, *, tm, tiles_per_seq, final_norm):
    i = pl.program_id(0)
    j = pl.program_id(1)

    @pl.when(j == 0)
    def _():
        keep = (i % tiles_per_seq != 0).astype(F32)
        hn_ref[0:HALO, :] = (_rms_rows(halo_ref[...], nw_ref[...]) * keep).astype(BF16)
        hn_ref[HALO:, :] = _rms_rows(h_ref[...], nw_ref[...]).astype(BF16)
        o_ref[...] = h_ref[...]

    hn = hn_ref[...]

    def conv(w_ref, cw_ref, cb_ref):
        u = jnp.dot(hn, w_ref[...], preferred_element_type=F32)
        acc = cb_ref[...]
        for tap in range(CONV_W):
            lo = HALO - (CONV_W - 1) + tap
            acc = acc + u[lo:lo + tm, :] * cw_ref[tap:tap + 1, :]
        return acc

    val = conv(wv_ref, cwv_ref, cbv_ref)
    gate = conv(wg_ref, cwg_ref, cbg_ref)
    act = (gate * jax.nn.sigmoid(gate) * val).astype(BF16)
    o_ref[...] += jnp.dot(act, wd_ref[...], preferred_element_type=F32)

    if final_norm:
        @pl.when(j == pl.num_programs(1) - 1)
        def _():
            o_ref[...] = _rms_rows(o_ref[...], fnw_ref[...])


def _ffn(h, nw, wv, wg, cwv, cwg, cbv, cbg, wd, fnw, seq, tm, final_norm):
    t = h.shape[0]
    tf = FFN_TILE
    kern = functools.partial(_ffn_kernel, tm=tm, tiles_per_seq=seq // tm, final_norm=final_norm)
    halo_blocks = tm // HALO
    const2 = lambda i, j: (0, 0)
    col = lambda i, j: (0, j)
    return pl.pallas_call(
        kern,
        grid=(t // tm, D_FF_PAD // tf),
        in_specs=[
            pl.BlockSpec((tm, D_MODEL), lambda i, j: (i, 0), pipeline_mode=pl.Buffered(1)),
            pl.BlockSpec((HALO, D_MODEL), lambda i, j: (jnp.maximum(i * halo_blocks - 1, 0), 0)),
            pl.BlockSpec((1, D_MODEL), const2),
            pl.BlockSpec((D_MODEL, tf), col),
            pl.BlockSpec((D_MODEL, tf), col),
            pl.BlockSpec((CONV_W, tf), col),
            pl.BlockSpec((CONV_W, tf), col),
            pl.BlockSpec((1, tf), col),
            pl.BlockSpec((1, tf), col),
            pl.BlockSpec((tf, D_MODEL), lambda i, j: (j, 0)),
            pl.BlockSpec((1, D_MODEL), const2),
        ],
        out_specs=pl.BlockSpec((tm, D_MODEL), lambda i, j: (i, 0)),
        out_shape=jax.ShapeDtypeStruct((t, D_MODEL), F32),
        scratch_shapes=[pltpu.VMEM((HALO + tm, D_MODEL), BF16)],
        compiler_params=_cparams(2),
        name="ffn",
    )(h, h, nw, wv, wg, cwv, cwg, cbv, cbg, wd, fnw)


def _pad_cols(a, n):
    return jnp.pad(a, ((0, 0), (0, n - a.shape[1])))


def _tile(n, pref):
    return pref if n % pref == 0 else n


def kernel(x, norm_mix_w, w_in, ssm_a_re, ssm_a_im, ssm_log_dt, ssm_b_re, ssm_b_im, ssm_c_re, ssm_c_im, ssm_d, ssm_w_glu, gla_w_gate, gla_b_gate, gla_norm_w, gla_w_br, diff_lam_q1, diff_lam_k1, diff_lam_q2, diff_lam_k2, diff_norm_w, diff_w_br, w_out, norm_ffn_w, ffn_w_up, ffn_conv_w, ffn_conv_b, ffn_w_down, norm_final_w):
    bsz, seq, _ = x.shape
    t = bsz * seq
    h = x.reshape(t, D_MODEL).astype(F32)
    row = lambda v: v.reshape(1, -1).astype(F32)

    for l in range(DEPTH):
        lambda_init = 0.8 - 0.6 * math.exp(-0.3 * l)
        w = w_in[l]
        w_main = jnp.concatenate([w[:, :ALR_ORIG], w[:, ALR_ORIG + GLA_GATE_RANK:]], axis=1).astype(BF16)
        w_alr = _pad_cols(w[:, ALR_ORIG:ALR_ORIG + GLA_GATE_RANK], LANES).astype(BF16)
        proj, alr = _inproj(h, row(norm_mix_w[l]), w_main, w_alr, _tile(t, 1024), 1024)

        bw, cw, are, aim = _s5_weights(ssm_a_re[l], ssm_a_im[l], ssm_log_dt[l], ssm_b_re[l],
                                       ssm_b_im[l], ssm_c_re[l], ssm_c_im[l])
        gy = _s5(proj, bw, cw, are, aim, row(ssm_d[l]), bsz, seq, _tile(seq, 256))

        wg = jnp.pad(gla_w_gate[l], ((0, LANES - GLA_GATE_RANK), (0, 0))).astype(BF16)
        ob = _gla(proj, alr, wg, row(gla_b_gate[l]), row(gla_norm_w[l]), bsz, seq, _tile(seq, 256))

        oc = _attn(proj, row(diff_lam_q1[l]), row(diff_lam_k1[l]), row(diff_lam_q2[l]),
                   row(diff_lam_k2[l]), row(diff_norm_w[l]), bsz, seq, _tile(seq, 1024), lambda_init)

        h = _merge(gy, ob, oc, proj, ssm_w_glu[l].astype(BF16), gla_w_br[l].astype(BF16),
                   diff_w_br[l].astype(BF16), w_out[l].astype(BF16), h, _tile(t, 512), 512)

        wup = ffn_w_up[l]
        cwf = ffn_conv_w[l].astype(F32)
        cbf = ffn_conv_b[l].reshape(1, -1).astype(F32)
        h = _ffn(h, row(norm_ffn_w[l]),
                 _pad_cols(wup[:, :D_FF], D_FF_PAD).astype(BF16),
                 _pad_cols(wup[:, D_FF:], D_FF_PAD).astype(BF16),
                 _pad_cols(cwf[:, :D_FF], D_FF_PAD), _pad_cols(cwf[:, D_FF:], D_FF_PAD),
                 _pad_cols(cbf[:, :D_FF], D_FF_PAD), _pad_cols(cbf[:, D_FF:], D_FF_PAD),
                 jnp.pad(ffn_w_down[l], ((0, D_FF_PAD - D_FF), (0, 0))).astype(BF16),
                 row(norm_final_w), seq, _tile(seq, 1024), l == DEPTH - 1)

    return h.reshape(bsz, seq, D_MODEL).astype(x.dtype)
```

```python
import functools
import math

import jax
import jax.numpy as jnp
import numpy as np
from jax import lax
from jax.experimental import pallas as pl
from jax.experimental.pallas import tpu as pltpu

F32 = jnp.float32
BF16 = jnp.bfloat16

D_MODEL = 2048
DEPTH = 2
D_SSM = 1024
SSM_GROUP = 16
N_SSM_GROUPS = D_SSM // SSM_GROUP
SSM_STATE = 64
GLA_HEADS = 4
GLA_DK = 128
GLA_DV = 256
GLA_GATE_RANK = 16
GLA_GATE_NORM = 16.0
GLA_CHUNK = 32
DIFF_HEADS = 4
DIFF_DK = 128
DIFF_DV = 256
N_BRANCH = 3
D_FF = 5504
CONV_W = 3
EPS = 1e-6
ALIBI_SLOPES = tuple(2.0 ** (-8.0 * (h + 1) / DIFF_HEADS) for h in range(DIFF_HEADS))
LOG2E = math.log2(math.e)

LANES = 128
SUBLANES = 8
BF16_ROWS = 16

COL_U = 0
COL_QG = COL_U + D_SSM
COL_KG = COL_QG + GLA_HEADS * GLA_DK
COL_VG = COL_KG + GLA_HEADS * GLA_DK
COL_RG = COL_VG + GLA_HEADS * GLA_DV
COL_QD = COL_RG + GLA_HEADS * GLA_DV
COL_KD = COL_QD + DIFF_HEADS * 2 * DIFF_DK
COL_VD = COL_KD + DIFF_HEADS * 2 * DIFF_DK
COL_GT = COL_VD + DIFF_HEADS * DIFF_DV
N_MAIN = COL_GT + N_BRANCH * D_MODEL
ALR_ORIG = COL_QD

D_FF_PAD = 5632
FFN_TILE = 512
FFN_SUBTILE = 256
HALO = BF16_ROWS

VMEM_LIMIT = 56 * 1024 * 1024


def _cparams(n_axes, flags=None):
    return pltpu.CompilerParams(
        dimension_semantics=("arbitrary",) * n_axes, vmem_limit_bytes=VMEM_LIMIT, flags=flags)


def _rms_rows(x, w):
    ms = jnp.mean(x * x, axis=-1, keepdims=True)
    return x * lax.rsqrt(ms + EPS) * w


def _inproj_kernel(h_ref, nw_ref, w_ref, walr_ref, cs_ref, o_ref, alr_ref, hn_ref):
    @pl.when(pl.program_id(1) == 0)
    def _():
        hn = _rms_rows(h_ref[...], nw_ref[...]).astype(BF16)
        hn_ref[...] = hn
        alr_ref[...] = jnp.dot(hn, walr_ref[...], preferred_element_type=F32).astype(alr_ref.dtype)

    acc = jnp.dot(hn_ref[...], w_ref[...], preferred_element_type=F32)
    o_ref[...] = (acc * cs_ref[...]).astype(o_ref.dtype)


def _inproj(h, nw, w_main, w_alr, tm, tn):
    t = h.shape[0]
    cs = jnp.ones((1, N_MAIN), F32).at[:, COL_QD:COL_KD].set(DIFF_DK ** -0.5 * LOG2E)
    return pl.pallas_call(
        _inproj_kernel,
        grid=(t // tm, N_MAIN // tn),
        in_specs=[
            pl.BlockSpec((tm, D_MODEL), lambda i, j: (i, 0)),
            pl.BlockSpec((1, D_MODEL), lambda i, j: (0, 0)),
            pl.BlockSpec((D_MODEL, tn), lambda i, j: (0, j)),
            pl.BlockSpec((D_MODEL, LANES), lambda i, j: (0, 0)),
            pl.BlockSpec((1, tn), lambda i, j: (0, j)),
        ],
        out_specs=[
            pl.BlockSpec((tm, tn), lambda i, j: (i, j)),
            pl.BlockSpec((tm, LANES), lambda i, j: (i, 0)),
        ],
        out_shape=[
            jax.ShapeDtypeStruct((t, N_MAIN), BF16),
            jax.ShapeDtypeStruct((t, LANES), BF16),
        ],
        scratch_shapes=[pltpu.VMEM((tm, D_MODEL), BF16)],
        compiler_params=_cparams(2),
        name="inproj",
    )(h, nw, w_main, w_alr, cs)


S5_CHUNKS = D_SSM // LANES
S5_SLABS = N_SSM_GROUPS * SSM_STATE // LANES
S5_SLABS_PER_CHUNK = S5_SLABS // S5_CHUNKS
S5_VREGS = S5_SLABS // SUBLANES


def _s5_kernel(u_ref, bw_ref, cw_ref, are_ref, aim_ref, d_ref, o_ref, xs_ref, st_ref, *, nb, tc, pitch):
    @pl.when(pl.program_id(0) == 0)
    def _():
        st_ref[...] = jnp.zeros_like(st_ref)

    half = S5_SLABS_PER_CHUNK * LANES
    a_re = [are_ref[k * SUBLANES:(k + 1) * SUBLANES, :] for k in range(S5_VREGS)]
    a_im = [aim_ref[k * SUBLANES:(k + 1) * SUBLANES, :] for k in range(S5_VREGS)]

    def slab(b, r):
        return (b * 2 * S5_SLABS + r) * pitch

    def drive(b):
        for c in range(S5_CHUNKS):
            x = jnp.dot(u_ref[b, :, c * LANES:(c + 1) * LANES], bw_ref[c], preferred_element_type=F32)
            for q in range(S5_SLABS_PER_CHUNK):
                r = c * S5_SLABS_PER_CHUNK + q
                xs_ref[pl.ds(slab(b, r), tc), :] = x[:, q * LANES:(q + 1) * LANES]
                xs_ref[pl.ds(slab(b, S5_SLABS + r), tc), :] = x[:, half + q * LANES:half + (q + 1) * LANES]

    def scan(b):
        st0 = b * 2 * S5_SLABS
        state = []
        for k in range(S5_VREGS):
            state.append([st_ref[st0 + k * SUBLANES:st0 + (k + 1) * SUBLANES, :],
                          st_ref[st0 + S5_SLABS + k * SUBLANES:st0 + S5_SLABS + (k + 1) * SUBLANES, :]])
        for t in range(tc):
            for k in range(S5_VREGS):
                re, im = state[k]
                idx_re = pl.ds(slab(b, k * SUBLANES) + t, SUBLANES, stride=pitch)
                idx_im = pl.ds(slab(b, S5_SLABS + k * SUBLANES) + t, SUBLANES, stride=pitch)
                nre = a_re[k] * re - a_im[k] * im + xs_ref[idx_re, :]
                nim = a_re[k] * im + a_im[k] * re + xs_ref[idx_im, :]
                xs_ref[idx_re, :] = nre
                xs_ref[idx_im, :] = nim
                state[k] = [nre, nim]
        for k in range(S5_VREGS):
            st_ref[st0 + k * SUBLANES:st0 + (k + 1) * SUBLANES, :] = state[k][0]
            st_ref[st0 + S5_SLABS + k * SUBLANES:st0 + S5_SLABS + (k + 1) * SUBLANES, :] = state[k][1]

    def readout(b):
        for c in range(S5_CHUNKS):
            parts = []
            for base in (0, S5_SLABS):
                for q in range(S5_SLABS_PER_CHUNK):
                    r = base + c * S5_SLABS_PER_CHUNK + q
                    parts.append(xs_ref[pl.ds(slab(b, r), tc), :].astype(BF16))
            hcat = jnp.concatenate(parts, axis=1)
            y = jnp.dot(hcat, cw_ref[c], preferred_element_type=F32)
            lanes = slice(c * LANES, (c + 1) * LANES)
            y = y + d_ref[:, lanes] * u_ref[b, :, lanes].astype(F32)
            o_ref[b, :, lanes] = jax.nn.gelu(y).astype(o_ref.dtype)

    for b in range(nb):
        drive(b)
    for b in range(nb):
        scan(b)
        readout(b)


def _s5_weights(a_re, a_im, log_dt, b_re, b_im, c_re, c_im):
    lam_r = jnp.minimum(a_re.astype(F32), -1e-4)
    lam_i = a_im.astype(F32)
    dt = jnp.exp(log_dt.astype(F32))[:, None]
    mag = jnp.exp(dt * lam_r)
    ab_r = mag * jnp.cos(dt * lam_i)
    ab_i = mag * jnp.sin(dt * lam_i)
    den = lam_r * lam_r + lam_i * lam_i
    f_r = ((ab_r - 1.0) * lam_r + ab_i * lam_i) / den
    f_i = (ab_i * lam_r - (ab_r - 1.0) * lam_i) / den
    br = b_re.astype(F32)
    bi = b_im.astype(F32)
    bb_r = f_r[..., None] * br - f_i[..., None] * bi
    bb_i = f_r[..., None] * bi + f_i[..., None] * br
    gpc = LANES // SSM_GROUP
    eye = jnp.eye(gpc, dtype=F32)

    def drive(bb):
        bb = bb.reshape(S5_CHUNKS, gpc, SSM_STATE, SSM_GROUP)
        m = jnp.einsum('cgpk,gh->cgkhp', bb, eye)
        return m.reshape(S5_CHUNKS, LANES, gpc * SSM_STATE)

    def readout(cc):
        cc = cc.astype(F32).reshape(S5_CHUNKS, gpc, SSM_GROUP, SSM_STATE)
        m = jnp.einsum('cgkp,gh->cgphk', cc, eye)
        return m.reshape(S5_CHUNKS, gpc * SSM_STATE, LANES)

    bw = jnp.concatenate([drive(bb_r), drive(bb_i)], axis=2).astype(BF16)
    cw = jnp.concatenate([readout(c_re), -readout(c_im)], axis=1).astype(BF16)
    return bw, cw, ab_r.reshape(S5_SLABS, LANES), ab_i.reshape(S5_SLABS, LANES)


def _s5(proj, bw, cw, are, aim, d_skip, bsz, seq, tc):
    nt = seq // tc
    pitch = tc + SUBLANES
    kern = functools.partial(_s5_kernel, nb=bsz, tc=tc, pitch=pitch)
    const3 = lambda i: (0, 0, 0)
    const2 = lambda i: (0, 0)
    out = pl.pallas_call(
        kern,
        grid=(nt,),
        in_specs=[
            pl.BlockSpec((bsz, tc, D_SSM), lambda i: (0, i, COL_U // D_SSM)),
            pl.BlockSpec(bw.shape, const3),
            pl.BlockSpec(cw.shape, const3),
            pl.BlockSpec(are.shape, const2),
            pl.BlockSpec(aim.shape, const2),
            pl.BlockSpec((1, D_SSM), const2),
        ],
        out_specs=pl.BlockSpec((bsz, tc, D_SSM), lambda i: (0, i, 0)),
        out_shape=jax.ShapeDtypeStruct((bsz, seq, D_SSM), BF16),
        scratch_shapes=[
            pltpu.VMEM((bsz * 2 * S5_SLABS * pitch, LANES), F32),
            pltpu.VMEM((bsz * 2 * S5_SLABS, LANES), F32),
        ],
        compiler_params=_cparams(1),
        name="s5",
    )(proj.reshape(bsz, seq, N_MAIN), bw, cw, are, aim, d_skip)
    return out.reshape(bsz * seq, D_SSM)


def _split3(x):
    hi = x.astype(BF16)
    r1 = x - hi.astype(F32)
    mid = r1.astype(BF16)
    lo = (r1 - mid.astype(F32)).astype(BF16)
    return hi, mid, lo


def _gla_kernel(q_ref, k_ref, v_ref, r_ref, alr_ref, wg_ref, bg_ref, nw_ref, mt_ref, mo_ref,
                o_ref, cum_ref, last_ref, st_ref, qd_ref, ki_ref, kt_ref, u_ref, sb_ref, *, tc):
    @pl.when(pl.program_id(1) == 0)
    def _():
        st_ref[...] = jnp.zeros_like(st_ref)

    z = jnp.dot(alr_ref[...], wg_ref[...], preferred_element_type=F32) + bg_ref[...]
    log_a = (jnp.minimum(z, 0.0) - jnp.log1p(jnp.exp(-jnp.abs(z)))) * (1.0 / GLA_GATE_NORM)
    parts = _split3(log_a)
    mt = mt_ref[...]
    mo = mo_ref[...]
    cum_ref[...] = sum(jnp.dot(mt, p, preferred_element_type=F32) for p in parts)
    last_ref[...] = sum(jnp.dot(mo, p, preferred_element_type=F32) for p in parts)

    rows_i = lax.broadcasted_iota(jnp.int32, (GLA_CHUNK, GLA_CHUNK), 0)
    cols_i = lax.broadcasted_iota(jnp.int32, (GLA_CHUNK, GLA_CHUNK), 1)
    tri = rows_i >= cols_i
    scale = GLA_DK ** -0.5
    nt_dims = (((1,), (1,)), ((), ()))
    tn_dims = (((0,), (0,)), ((), ()))

    n_chunks = tc // GLA_CHUNK
    cells = [(n, hd) for n in range(n_chunks) for hd in range(GLA_HEADS)]
    rows_of = lambda n: slice(n * GLA_CHUNK, (n + 1) * GLA_CHUNK)
    kc_of = lambda hd: slice(hd * GLA_DK, (hd + 1) * GLA_DK)
    vc_of = lambda hd: slice(hd * GLA_DV, (hd + 1) * GLA_DV)

    cum = cum_ref[...]
    last = last_ref[...]
    kf = k_ref[...].astype(F32)
    qd_ref[...] = (q_ref[...].astype(F32) * scale * jnp.exp(cum)).astype(BF16)
    ki_ref[...] = (kf * jnp.exp(-cum)).astype(BF16)
    kt_ref[...] = (kf * jnp.exp(last - cum)).astype(BF16)

    sc = {}
    for n, hd in cells:
        s = lax.dot_general(qd_ref[rows_of(n), kc_of(hd)], ki_ref[rows_of(n), kc_of(hd)], nt_dims,
                            preferred_element_type=F32)
        sc[n, hd] = jnp.where(tri, s, 0.0).astype(BF16)
    for n, hd in cells:
        u_ref[n, hd] = lax.dot_general(v_ref[rows_of(n), vc_of(hd)], kt_ref[rows_of(n), kc_of(hd)],
                                       tn_dims, preferred_element_type=F32)
    for hd in range(GLA_HEADS):
        st = st_ref[hd]
        for n in range(n_chunks):
            sb_ref[n, hd] = st.astype(BF16)
            dl = jnp.exp(last_ref[n * GLA_CHUNK:n * GLA_CHUNK + 1, kc_of(hd)])
            st = dl * st + u_ref[n, hd]
        st_ref[hd] = st
    for n, hd in cells:
        vv = v_ref[rows_of(n), vc_of(hd)]
        o = jnp.dot(sc[n, hd], vv, preferred_element_type=F32)
        o = o + lax.dot_general(qd_ref[rows_of(n), kc_of(hd)], sb_ref[n, hd], nt_dims,
                                preferred_element_type=F32)
        on = _rms_rows(o, nw_ref[...])
        rr = r_ref[rows_of(n), vc_of(hd)].astype(F32)
        o_ref[rows_of(n), vc_of(hd)] = (on * (rr * jax.nn.sigmoid(rr))).astype(o_ref.dtype)


def _gla(proj, alr, wg, bg, nw, bsz, seq, tc):
    nt = seq // tc
    hk = GLA_HEADS * GLA_DK
    hv = GLA_HEADS * GLA_DV
    pos = np.arange(tc)
    same = (pos[:, None] // GLA_CHUNK) == (pos[None, :] // GLA_CHUNK)
    mt = jnp.asarray(same & (pos[:, None] >= pos[None, :]), BF16)
    mo = jnp.asarray(same, BF16)
    kern = functools.partial(_gla_kernel, tc=tc)
    const2 = lambda b, i: (0, 0)
    row = lambda col: (lambda b, i: (b * nt + i, col))
    return pl.pallas_call(
        kern,
        grid=(bsz, nt),
        in_specs=[
            pl.BlockSpec((tc, hk), row(COL_QG // hk)),
            pl.BlockSpec((tc, hk), row(COL_KG // hk)),
            pl.BlockSpec((tc, hv), row(COL_VG // hv)),
            pl.BlockSpec((tc, hv), row(COL_RG // hv)),
            pl.BlockSpec((tc, LANES), row(0)),
            pl.BlockSpec((LANES, hk), const2),
            pl.BlockSpec((1, hk), const2),
            pl.BlockSpec((1, GLA_DV), const2),
            pl.BlockSpec((tc, tc), const2),
            pl.BlockSpec((tc, tc), const2),
        ],
        out_specs=pl.BlockSpec((tc, hv), row(0)),
        out_shape=jax.ShapeDtypeStruct((bsz * seq, hv), BF16),
        scratch_shapes=[
            pltpu.VMEM((tc, hk), F32),
            pltpu.VMEM((tc, hk), F32),
            pltpu.VMEM((GLA_HEADS, GLA_DV, GLA_DK), F32),
            pltpu.VMEM((tc, hk), BF16),
            pltpu.VMEM((tc, hk), BF16),
            pltpu.VMEM((tc, hk), BF16),
            pltpu.VMEM((tc // GLA_CHUNK, GLA_HEADS, GLA_DV, GLA_DK), F32),
            pltpu.VMEM((tc // GLA_CHUNK, GLA_HEADS, GLA_DV, GLA_DK), BF16),
        ],
        compiler_params=_cparams(2),
        name="gla",
    )(proj, proj, proj, proj, alr, wg, bg, nw, mt, mo)


ATTN_ROW_CHUNK = 16


def _attn_kernel(q_ref, k_ref, v_ref, pos_ref, one_ref, lq1_ref, lk1_ref, lq2_ref, lk2_ref, nw_ref,
                 o_ref, qa_ref, s_ref, p_ref, acc_ref, m_ref, l_ref, al_ref,
                 *, tq, lambda_init):
    head = pl.program_id(1)
    qi = pl.program_id(2)
    slope2 = jnp.float32(ALIBI_SLOPES[-1] * LOG2E)
    for hd in range(DIFF_HEADS - 1):
        slope2 = jnp.where(head == hd, jnp.float32(ALIBI_SLOPES[hd] * LOG2E), slope2)
    nt_dims = (((1,), (1,)), ((), ()))
    n_lane_chunks = tq // LANES

    for m in range(2):
        qa_ref[m] = jnp.concatenate([q_ref[:, m * DIFF_DK:(m + 1) * DIFF_DK], one_ref[...]], axis=1)
    acc_ref[...] = jnp.zeros_like(acc_ref)
    l_ref[...] = jnp.zeros_like(l_ref)
    m_ref[...] = jnp.full(m_ref.shape, -jnp.inf, F32)
    pos = pos_ref[0]

    def tile_rows(j):
        return pl.ds(pl.multiple_of(j * tq, tq), tq)

    def tile_bias(j):
        return -slope2 * ((qi - j) * tq).astype(F32)

    def scores(m, j, masked):
        k_aug = jnp.concatenate([k_ref[tile_rows(j), m * DIFF_DK:(m + 1) * DIFF_DK], pos], axis=1)
        s = lax.dot_general(qa_ref[m], k_aug, nt_dims, preferred_element_type=F32)
        if masked:
            keep = (lax.broadcasted_iota(jnp.int32, (tq, tq), 0)
                    >= lax.broadcasted_iota(jnp.int32, (tq, tq), 1))
            s = jnp.where(keep, s, -jnp.inf)
        s_ref[m] = s
        mx = jnp.broadcast_to(jnp.max(s, axis=-1, keepdims=True), (tq, LANES))
        m_prev = m_ref[m]
        m_new = jnp.maximum(m_prev, mx + tile_bias(j))
        al_ref[m] = jnp.exp2(m_prev - m_new)
        m_ref[m] = m_new

    def accumulate(m, j):
        tb = tile_bias(j)
        for c in range(tq // ATTN_ROW_CHUNK):
            rr = pl.ds(c * ATTN_ROW_CHUNK, ATTN_ROW_CHUNK)
            sh = m_ref[m, rr, :] - tb
            for lc in range(n_lane_chunks):
                lanes = slice(lc * LANES, (lc + 1) * LANES)
                p_ref[m, rr, lanes] = jnp.exp2(s_ref[m, rr, lanes] - sh).astype(BF16)
        for c in range(tq // ATTN_ROW_CHUNK):
            rr = pl.ds(c * ATTN_ROW_CHUNK, ATTN_ROW_CHUNK)
            part = None
            for lc in range(n_lane_chunks):
                x = p_ref[m, rr, lc * LANES:(lc + 1) * LANES].astype(F32)
                part = x if part is None else part + x
            l_ref[m, rr, :] = al_ref[m, rr, :] * l_ref[m, rr, :] + part
        pv = jnp.dot(p_ref[m], v_ref[tile_rows(j), :], preferred_element_type=F32)
        al = al_ref[m]
        for hv in range(DIFF_DV // LANES):
            lanes = slice(hv * LANES, (hv + 1) * LANES)
            acc_ref[m, :, lanes] = al * acc_ref[m, :, lanes] + pv[:, lanes]

    def step(j, mask_next):
        accumulate(0, j)
        scores(1, j, False)
        accumulate(1, j)
        scores(0, j + 1, mask_next)

    def diagonal_tile():
        accumulate(0, qi)
        scores(1, qi, True)
        accumulate(1, qi)

    @pl.when(qi == 0)
    def _():
        scores(0, 0, True)
        diagonal_tile()

    @pl.when(qi > 0)
    def _():
        scores(0, 0, False)

    def body(j, carry):
        step(j, False)
        return carry

    lax.fori_loop(0, qi - 1, body, 0)

    @pl.when(qi > 0)
    def _():
        step(qi - 1, True)
        diagonal_tile()

    lam = (jnp.exp(jnp.sum(lq1_ref[...] * lk1_ref[...], axis=-1, keepdims=True))
           - jnp.exp(jnp.sum(lq2_ref[...] * lk2_ref[...], axis=-1, keepdims=True)) + lambda_init)
    l1 = jnp.sum(l_ref[0], axis=-1, keepdims=True)
    l2 = jnp.sum(l_ref[1], axis=-1, keepdims=True)
    o = acc_ref[0] / l1 - lam * (acc_ref[1] / l2)
    o = _rms_rows(o, nw_ref[...]) * (1.0 - lambda_init)
    o_ref[...] = o.astype(o_ref.dtype)


def _attn_pos_columns(tq):
    slopes2 = np.asarray([np.float32(s * LOG2E) for s in ALIBI_SLOPES], np.float32)
    rest = np.arange(tq, dtype=np.float32)[None, :] * slopes2[:, None]
    pos = np.zeros((DIFF_HEADS, tq, LANES), BF16)
    one = np.zeros((tq, LANES), BF16)
    for col in range(3):
        part = rest.astype(BF16)
        pos[:, :, col] = part
        one[:, col] = 1.0
        rest = rest - part.astype(np.float32)
    return jnp.asarray(pos), jnp.asarray(one)


def _attn(proj, lq1, lk1, lq2, lk2, nw, bsz, seq, tq, lambda_init):
    nq = seq // tq
    w2 = 2 * DIFF_DK
    kern = functools.partial(_attn_kernel, tq=tq, lambda_init=lambda_init)
    const2 = lambda b, h, i: (0, 0)
    pos, one = _attn_pos_columns(tq)
    stat = pltpu.VMEM((2, tq, LANES), F32)
    return pl.pallas_call(
        kern,
        grid=(bsz, DIFF_HEADS, nq),
        in_specs=[
            pl.BlockSpec((tq, w2), lambda b, h, i: (b * nq + i, COL_QD // w2 + h)),
            pl.BlockSpec((seq, w2), lambda b, h, i: (b, COL_KD // w2 + h),
                         pipeline_mode=pl.Buffered(1)),
            pl.BlockSpec((seq, DIFF_DV), lambda b, h, i: (b, COL_VD // DIFF_DV + h),
                         pipeline_mode=pl.Buffered(1)),
            pl.BlockSpec((1, tq, LANES), lambda b, h, i: (h, 0, 0)),
            pl.BlockSpec((tq, LANES), const2),
            pl.BlockSpec((1, DIFF_DK), const2),
            pl.BlockSpec((1, DIFF_DK), const2),
            pl.BlockSpec((1, DIFF_DK), const2),
            pl.BlockSpec((1, DIFF_DK), const2),
            pl.BlockSpec((1, DIFF_DV), const2),
        ],
        out_specs=pl.BlockSpec((tq, DIFF_DV), lambda b, h, i: (b * nq + i, h)),
        out_shape=jax.ShapeDtypeStruct((bsz * seq, DIFF_HEADS * DIFF_DV), BF16),
        scratch_shapes=[
            pltpu.VMEM((2, tq, 2 * DIFF_DK), BF16),
            pltpu.VMEM((2, tq, tq), F32),
            pltpu.VMEM((2, tq, tq), BF16),
            pltpu.VMEM((2, tq, DIFF_DV), F32),
            stat, stat, stat,
        ],
        compiler_params=_cparams(3),
        name="diffattn",
    )(proj, proj, proj, pos, one, lq1, lk1, lq2, lk2, nw)


def _merge_kernel(gy_ref, ob_ref, oc_ref, g0_ref, g1_ref, g2_ref, wgv_ref, wgg_ref, wb_ref, wc_ref,
                  wo_ref, h_ref, o_ref, acc_ref):
    j = pl.program_id(1)

    @pl.when(j == 0)
    def _():
        acc_ref[...] = jnp.zeros_like(acc_ref)

    gy = gy_ref[...]
    val = jnp.dot(gy, wgv_ref[...], preferred_element_type=F32)
    gate = jnp.dot(gy, wgg_ref[...], preferred_element_type=F32)
    ya = val * jax.nn.sigmoid(gate)
    yb = jnp.dot(ob_ref[...], wb_ref[...], preferred_element_type=F32)
    yc = jnp.dot(oc_ref[...], wc_ref[...], preferred_element_type=F32)
    merged = (jax.nn.sigmoid(g0_ref[...].astype(F32)) * ya
              + jax.nn.sigmoid(g1_ref[...].astype(F32)) * yb
              + jax.nn.sigmoid(g2_ref[...].astype(F32)) * yc)
    acc_ref[...] += jnp.dot(merged.astype(BF16), wo_ref[...], preferred_element_type=F32)

    @pl.when(j == pl.num_programs(1) - 1)
    def _():
        o_ref[...] = h_ref[...] + acc_ref[...]


def _merge(gy, ob, oc, proj, w_glu, wbr_g, wbr_d, w_out, h, tm, tn):
    t = h.shape[0]
    nj = D_MODEL // tn
    gt0 = COL_GT // tn
    left = lambda i, j: (i, 0)
    return pl.pallas_call(
        _merge_kernel,
        grid=(t // tm, nj),
        in_specs=[
            pl.BlockSpec((tm, D_SSM), left),
            pl.BlockSpec((tm, GLA_HEADS * GLA_DV), left),
            pl.BlockSpec((tm, DIFF_HEADS * DIFF_DV), left),
            pl.BlockSpec((tm, tn), lambda i, j: (i, gt0 + j)),
            pl.BlockSpec((tm, tn), lambda i, j: (i, gt0 + nj + j)),
            pl.BlockSpec((tm, tn), lambda i, j: (i, gt0 + 2 * nj + j)),
            pl.BlockSpec((D_SSM, tn), lambda i, j: (0, j)),
            pl.BlockSpec((D_SSM, tn), lambda i, j: (0, nj + j)),
            pl.BlockSpec((GLA_HEADS * GLA_DV, tn), lambda i, j: (0, j)),
            pl.BlockSpec((DIFF_HEADS * DIFF_DV, tn), lambda i, j: (0, j)),
            pl.BlockSpec((tn, D_MODEL), lambda i, j: (j, 0)),
            pl.BlockSpec((tm, D_MODEL), left),
        ],
        out_specs=pl.BlockSpec((tm, D_MODEL), left),
        out_shape=jax.ShapeDtypeStruct((t, D_MODEL), F32),
        scratch_shapes=[pltpu.VMEM((tm, D_MODEL), F32)],
        compiler_params=_cparams(2),
        name="merge",
    )(gy, ob, oc, proj, proj, proj, w_glu, w_glu, wbr_g, wbr_d, w_out, h)


def _ffn_kernel(h_ref, halo_ref, nw_ref, wv_ref, wg_ref, cwv_ref, cwg_ref, cbv_ref, cbg_ref, wd_ref,
                fnw_ref, o_ref, hn_ref, *, tm, tiles_per_seq, final_norm):
    i = pl.program_id(0)
    j = pl.program_id(1)

    @pl.when(j == 0)
    def _():
        keep = (i % tiles_per_seq != 0).astype(F32)
        hn_ref[0:HALO, :] = (_rms_rows(halo_ref[...], nw_ref[...]) * keep).astype(BF16)
        hn_ref[HALO:, :] = _rms_rows(h_ref[...], nw_ref[...]).astype(BF16)
        o_ref[...] = h_ref[...]

    hn = hn_ref[...]

    def conv(u, cw_ref, cb_ref, cols):
        acc = cb_ref[:, cols]
        for tap in range(CONV_W):
            lo = HALO - (CONV_W - 1) + tap
            acc = acc + u[lo:lo + tm, :] * cw_ref[tap:tap + 1, cols]
        return acc

    subs = [slice(s * FFN_SUBTILE, (s + 1) * FFN_SUBTILE) for s in range(FFN_TILE // FFN_SUBTILE)]
    ups = [(jnp.dot(hn, wv_ref[:, cols], preferred_element_type=F32),
            jnp.dot(hn, wg_ref[:, cols], preferred_element_type=F32)) for cols in subs]
    down = None
    for cols, (uv, ug) in zip(subs, ups):
        val = conv(uv, cwv_ref, cbv_ref, cols)
        gate = conv(ug, cwg_ref, cbg_ref, cols)
        act = (gate * jax.nn.sigmoid(gate) * val).astype(BF16)
        part = jnp.dot(act, wd_ref[cols, :], preferred_element_type=F32)
        down = part if down is None else down + part
    o_ref[...] += down

    if final_norm:
        @pl.when(j == pl.num_programs(1) - 1)
        def _():
            o_ref[...] = _rms_rows(o_ref[...], fnw_ref[...])


def _ffn(h, nw, wv, wg, cwv, cwg, cbv, cbg, wd, fnw, seq, tm, final_norm):
    t = h.shape[0]
    tf = FFN_TILE
    kern = functools.partial(_ffn_kernel, tm=tm, tiles_per_seq=seq // tm, final_norm=final_norm)
    halo_blocks = tm // HALO
    const2 = lambda i, j: (0, 0)
    col = lambda i, j: (0, j)
    return pl.pallas_call(
        kern,
        grid=(t // tm, D_FF_PAD // tf),
        in_specs=[
            pl.BlockSpec((tm, D_MODEL), lambda i, j: (i, 0), pipeline_mode=pl.Buffered(1)),
            pl.BlockSpec((HALO, D_MODEL), lambda i, j: (jnp.maximum(i * halo_blocks - 1, 0), 0)),
            pl.BlockSpec((1, D_MODEL), const2),
            pl.BlockSpec((D_MODEL, tf), col),
            pl.BlockSpec((D_MODEL, tf), col),
            pl.BlockSpec((CONV_W, tf), col),
            pl.BlockSpec((CONV_W, tf), col),
            pl.BlockSpec((1, tf), col),
            pl.BlockSpec((1, tf), col),
            pl.BlockSpec((tf, D_MODEL), lambda i, j: (j, 0)),
            pl.BlockSpec((1, D_MODEL), const2),
        ],
        out_specs=pl.BlockSpec((tm, D_MODEL), lambda i, j: (i, 0)),
        out_shape=jax.ShapeDtypeStruct((t, D_MODEL), F32),
        scratch_shapes=[pltpu.VMEM((HALO + tm, D_MODEL), BF16)],
        compiler_params=_cparams(2),
        name="ffn",
    )(h, h, nw, wv, wg, cwv, cwg, cbv, cbg, wd, fnw)


def _pad_cols(a, n):
    return jnp.pad(a, ((0, 0), (0, n - a.shape[1])))


def _tile(n, pref):
    return pref if n % pref == 0 else n


def kernel(x, norm_mix_w, w_in, ssm_a_re, ssm_a_im, ssm_log_dt, ssm_b_re, ssm_b_im, ssm_c_re, ssm_c_im, ssm_d, ssm_w_glu, gla_w_gate, gla_b_gate, gla_norm_w, gla_w_br, diff_lam_q1, diff_lam_k1, diff_lam_q2, diff_lam_k2, diff_norm_w, diff_w_br, w_out, norm_ffn_w, ffn_w_up, ffn_conv_w, ffn_conv_b, ffn_w_down, norm_final_w):
    bsz, seq, _ = x.shape
    t = bsz * seq
    h = x.reshape(t, D_MODEL).astype(F32)
    row = lambda v: v.reshape(1, -1).astype(F32)

    for l in range(DEPTH):
        lambda_init = 0.8 - 0.6 * math.exp(-0.3 * l)
        w = w_in[l]
        w_main = jnp.concatenate([w[:, :ALR_ORIG], w[:, ALR_ORIG + GLA_GATE_RANK:]], axis=1).astype(BF16)
        w_alr = _pad_cols(w[:, ALR_ORIG:ALR_ORIG + GLA_GATE_RANK], LANES).astype(BF16)
        proj, alr = _inproj(h, row(norm_mix_w[l]), w_main, w_alr, _tile(t, 1024), 1024)

        bw, cw, are, aim = _s5_weights(ssm_a_re[l], ssm_a_im[l], ssm_log_dt[l], ssm_b_re[l],
                                       ssm_b_im[l], ssm_c_re[l], ssm_c_im[l])
        gy = _s5(proj, bw, cw, are, aim, row(ssm_d[l]), bsz, seq, _tile(seq, 256))

        wg = jnp.pad(gla_w_gate[l], ((0, LANES - GLA_GATE_RANK), (0, 0))).astype(BF16)
        ob = _gla(proj, alr, wg, row(gla_b_gate[l]), row(gla_norm_w[l]), bsz, seq, _tile(seq, 256))

        oc = _attn(proj, row(diff_lam_q1[l]), row(diff_lam_k1[l]), row(diff_lam_q2[l]),
                   row(diff_lam_k2[l]), row(diff_norm_w[l]), bsz, seq, _tile(seq, 1024), lambda_init)

        h = _merge(gy, ob, oc, proj, ssm_w_glu[l].astype(BF16), gla_w_br[l].astype(BF16),
                   diff_w_br[l].astype(BF16), w_out[l].astype(BF16), h, _tile(t, 512), 512)

        wup = ffn_w_up[l]
        cwf = ffn_conv_w[l].astype(F32)
        cbf = ffn_conv_b[l].reshape(1, -1).astype(F32)
        h = _ffn(h, row(norm_ffn_w[l]),
                 _pad_cols(wup[:, :D_FF], D_FF_PAD).astype(BF16),
                 _pad_cols(wup[:, D_FF:], D_FF_PAD).astype(BF16),
                 _pad_cols(cwf[:, :D_FF], D_FF_PAD), _pad_cols(cwf[:, D_FF:], D_FF_PAD),
                 _pad_cols(cbf[:, :D_FF], D_FF_PAD), _pad_cols(cbf[:, D_FF:], D_FF_PAD),
                 jnp.pad(ffn_w_down[l], ((0, D_FF_PAD - D_FF), (0, 0))).astype(BF16),
                 row(norm_final_w), seq, _tile(seq, 1024), l == DEPTH - 1)

    return h.reshape(bsz, seq, D_MODEL).astype(x.dtype)
```

```python
import functools
import math

import jax
import jax.numpy as jnp
import numpy as np
from jax import lax
from jax.experimental import pallas as pl
from jax.experimental.pallas import tpu as pltpu

F32 = jnp.float32
BF16 = jnp.bfloat16

D_MODEL = 2048
DEPTH = 2
D_SSM = 1024
SSM_GROUP = 16
N_SSM_GROUPS = D_SSM // SSM_GROUP
SSM_STATE = 64
GLA_HEADS = 4
GLA_DK = 128
GLA_DV = 256
GLA_GATE_RANK = 16
GLA_GATE_NORM = 16.0
GLA_CHUNK = 32
DIFF_HEADS = 4
DIFF_DK = 128
DIFF_DV = 256
N_BRANCH = 3
D_FF = 5504
CONV_W = 3
EPS = 1e-6
ALIBI_SLOPES = tuple(2.0 ** (-8.0 * (h + 1) / DIFF_HEADS) for h in range(DIFF_HEADS))
LOG2E = math.log2(math.e)

LANES = 128
SUBLANES = 8
BF16_ROWS = 16

COL_U = 0
COL_QG = COL_U + D_SSM
COL_KG = COL_QG + GLA_HEADS * GLA_DK
COL_VG = COL_KG + GLA_HEADS * GLA_DK
COL_RG = COL_VG + GLA_HEADS * GLA_DV
COL_QD = COL_RG + GLA_HEADS * GLA_DV
COL_KD = COL_QD + DIFF_HEADS * 2 * DIFF_DK
COL_VD = COL_KD + DIFF_HEADS * 2 * DIFF_DK
COL_GT = COL_VD + DIFF_HEADS * DIFF_DV
N_MAIN = COL_GT + N_BRANCH * D_MODEL
ALR_ORIG = COL_QD

D_FF_PAD = 5632
FFN_TILE = 512
FFN_SUBTILE = 256
HALO = BF16_ROWS

VMEM_LIMIT = 56 * 1024 * 1024


def _cparams(n_axes, flags=None):
    return pltpu.CompilerParams(
        dimension_semantics=("arbitrary",) * n_axes, vmem_limit_bytes=VMEM_LIMIT, flags=flags)


def _rms_rows(x, w):
    ms = jnp.mean(x * x, axis=-1, keepdims=True)
    return x * lax.rsqrt(ms + EPS) * w


def _inproj_kernel(h_ref, nw_ref, w_ref, walr_ref, cs_ref, o_ref, alr_ref, hn_ref):
    @pl.when(pl.program_id(1) == 0)
    def _():
        hn = _rms_rows(h_ref[...], nw_ref[...]).astype(BF16)
        hn_ref[...] = hn
        alr_ref[...] = jnp.dot(hn, walr_ref[...], preferred_element_type=F32).astype(alr_ref.dtype)

    acc = jnp.dot(hn_ref[...], w_ref[...], preferred_element_type=F32)
    o_ref[...] = (acc * cs_ref[...]).astype(o_ref.dtype)


def _inproj(h, nw, w_main, w_alr, tm, tn):
    t = h.shape[0]
    cs = jnp.ones((1, N_MAIN), F32).at[:, COL_QD:COL_KD].set(DIFF_DK ** -0.5 * LOG2E)
    return pl.pallas_call(
        _inproj_kernel,
        grid=(t // tm, N_MAIN // tn),
        in_specs=[
            pl.BlockSpec((tm, D_MODEL), lambda i, j: (i, 0)),
            pl.BlockSpec((1, D_MODEL), lambda i, j: (0, 0)),
            pl.BlockSpec((D_MODEL, tn), lambda i, j: (0, j)),
            pl.BlockSpec((D_MODEL, LANES), lambda i, j: (0, 0)),
            pl.BlockSpec((1, tn), lambda i, j: (0, j)),
        ],
        out_specs=[
            pl.BlockSpec((tm, tn), lambda i, j: (i, j)),
            pl.BlockSpec((tm, LANES), lambda i, j: (i, 0)),
        ],
        out_shape=[
            jax.ShapeDtypeStruct((t, N_MAIN), BF16),
            jax.ShapeDtypeStruct((t, LANES), BF16),
        ],
        scratch_shapes=[pltpu.VMEM((tm, D_MODEL), BF16)],
        compiler_params=_cparams(2),
        name="inproj",
    )(h, nw, w_main, w_alr, cs)


S5_CHUNKS = D_SSM // LANES
S5_SLABS = N_SSM_GROUPS * SSM_STATE // LANES
S5_SLABS_PER_CHUNK = S5_SLABS // S5_CHUNKS
S5_VREGS = S5_SLABS // SUBLANES


def _s5_kernel(u_ref, bw_ref, cw_ref, are_ref, aim_ref, d_ref, o_ref, xs_ref, st_ref, *, nb, tc, pitch):
    @pl.when(pl.program_id(0) == 0)
    def _():
        st_ref[...] = jnp.zeros_like(st_ref)

    half = S5_SLABS_PER_CHUNK * LANES
    a_re = [are_ref[k * SUBLANES:(k + 1) * SUBLANES, :] for k in range(S5_VREGS)]
    a_im = [aim_ref[k * SUBLANES:(k + 1) * SUBLANES, :] for k in range(S5_VREGS)]

    def slab(b, r):
        return (b * 2 * S5_SLABS + r) * pitch

    def drive(b):
        for c in range(S5_CHUNKS):
            x = jnp.dot(u_ref[b, :, c * LANES:(c + 1) * LANES], bw_ref[c], preferred_element_type=F32)
            for q in range(S5_SLABS_PER_CHUNK):
                r = c * S5_SLABS_PER_CHUNK + q
                xs_ref[pl.ds(slab(b, r), tc), :] = x[:, q * LANES:(q + 1) * LANES]
                xs_ref[pl.ds(slab(b, S5_SLABS + r), tc), :] = x[:, half + q * LANES:half + (q + 1) * LANES]

    def scan(b):
        st0 = b * 2 * S5_SLABS
        state = []
        for k in range(S5_VREGS):
            state.append([st_ref[st0 + k * SUBLANES:st0 + (k + 1) * SUBLANES, :],
                          st_ref[st0 + S5_SLABS + k * SUBLANES:st0 + S5_SLABS + (k + 1) * SUBLANES, :]])
        for t in range(tc):
            for k in range(S5_VREGS):
                re, im = state[k]
                idx_re = pl.ds(slab(b, k * SUBLANES) + t, SUBLANES, stride=pitch)
                idx_im = pl.ds(slab(b, S5_SLABS + k * SUBLANES) + t, SUBLANES, stride=pitch)
                nre = a_re[k] * re - a_im[k] * im + xs_ref[idx_re, :]
                nim = a_re[k] * im + a_im[k] * re + xs_ref[idx_im, :]
                xs_ref[idx_re, :] = nre
                xs_ref[idx_im, :] = nim
                state[k] = [nre, nim]
        for k in range(S5_VREGS):
            st_ref[st0 + k * SUBLANES:st0 + (k + 1) * SUBLANES, :] = state[k][0]
            st_ref[st0 + S5_SLABS + k * SUBLANES:st0 + S5_SLABS + (k + 1) * SUBLANES, :] = state[k][1]

    def readout(b):
        for c in range(S5_CHUNKS):
            parts = []
            for base in (0, S5_SLABS):
                for q in range(S5_SLABS_PER_CHUNK):
                    r = base + c * S5_SLABS_PER_CHUNK + q
                    parts.append(xs_ref[pl.ds(slab(b, r), tc), :].astype(BF16))
            hcat = jnp.concatenate(parts, axis=1)
            y = jnp.dot(hcat, cw_ref[c], preferred_element_type=F32)
            lanes = slice(c * LANES, (c + 1) * LANES)
            y = y + d_ref[:, lanes] * u_ref[b, :, lanes].astype(F32)
            o_ref[b, :, lanes] = jax.nn.gelu(y).astype(o_ref.dtype)

    for b in range(nb):
        drive(b)
    for b in range(nb):
        scan(b)
        readout(b)


def _s5_weights(a_re, a_im, log_dt, b_re, b_im, c_re, c_im):
    lam_r = jnp.minimum(a_re.astype(F32), -1e-4)
    lam_i = a_im.astype(F32)
    dt = jnp.exp(log_dt.astype(F32))[:, None]
    mag = jnp.exp(dt * lam_r)
    ab_r = mag * jnp.cos(dt * lam_i)
    ab_i = mag * jnp.sin(dt * lam_i)
    den = lam_r * lam_r + lam_i * lam_i
    f_r = ((ab_r - 1.0) * lam_r + ab_i * lam_i) / den
    f_i = (ab_i * lam_r - (ab_r - 1.0) * lam_i) / den
    br = b_re.astype(F32)
    bi = b_im.astype(F32)
    bb_r = f_r[..., None] * br - f_i[..., None] * bi
    bb_i = f_r[..., None] * bi + f_i[..., None] * br
    gpc = LANES // SSM_GROUP
    eye = jnp.eye(gpc, dtype=F32)

    def drive(bb):
        bb = bb.reshape(S5_CHUNKS, gpc, SSM_STATE, SSM_GROUP)
        m = jnp.einsum('cgpk,gh->cgkhp', bb, eye)
        return m.reshape(S5_CHUNKS, LANES, gpc * SSM_STATE)

    def readout(cc):
        cc = cc.astype(F32).reshape(S5_CHUNKS, gpc, SSM_GROUP, SSM_STATE)
        m = jnp.einsum('cgkp,gh->cgphk', cc, eye)
        return m.reshape(S5_CHUNKS, gpc * SSM_STATE, LANES)

    bw = jnp.concatenate([drive(bb_r), drive(bb_i)], axis=2).astype(BF16)
    cw = jnp.concatenate([readout(c_re), -readout(c_im)], axis=1).astype(BF16)
    return bw, cw, ab_r.reshape(S5_SLABS, LANES), ab_i.reshape(S5_SLABS, LANES)


def _s5(proj, bw, cw, are, aim, d_skip, bsz, seq, tc):
    nt = seq // tc
    pitch = tc + SUBLANES
    kern = functools.partial(_s5_kernel, nb=bsz, tc=tc, pitch=pitch)
    const3 = lambda i: (0, 0, 0)
    const2 = lambda i: (0, 0)
    out = pl.pallas_call(
        kern,
        grid=(nt,),
        in_specs=[
            pl.BlockSpec((bsz, tc, D_SSM), lambda i: (0, i, COL_U // D_SSM)),
            pl.BlockSpec(bw.shape, const3),
            pl.BlockSpec(cw.shape, const3),
            pl.BlockSpec(are.shape, const2),
            pl.BlockSpec(aim.shape, const2),
            pl.BlockSpec((1, D_SSM), const2),
        ],
        out_specs=pl.BlockSpec((bsz, tc, D_SSM), lambda i: (0, i, 0)),
        out_shape=jax.ShapeDtypeStruct((bsz, seq, D_SSM), BF16),
        scratch_shapes=[
            pltpu.VMEM((bsz * 2 * S5_SLABS * pitch, LANES), F32),
            pltpu.VMEM((bsz * 2 * S5_SLABS, LANES), F32),
        ],
        compiler_params=_cparams(1),
        name="s5",
    )(proj.reshape(bsz, seq, N_MAIN), bw, cw, are, aim, d_skip)
    return out.reshape(bsz * seq, D_SSM)


def _split3(x):
    hi = x.astype(BF16)
    r1 = x - hi.astype(F32)
    mid = r1.astype(BF16)
    lo = (r1 - mid.astype(F32)).astype(BF16)
    return hi, mid, lo


def _gla_kernel(q_ref, k_ref, v_ref, r_ref, alr_ref, wg_ref, bg_ref, nw_ref, mt_ref, mo_ref,
                o_ref, cum_ref, last_ref, st_ref, qd_ref, ki_ref, kt_ref, u_ref, sb_ref, *, tc):
    @pl.when(pl.program_id(1) == 0)
    def _():
        st_ref[...] = jnp.zeros_like(st_ref)

    z = jnp.dot(alr_ref[...], wg_ref[...], preferred_element_type=F32) + bg_ref[...]
    log_a = (jnp.minimum(z, 0.0) - jnp.log1p(jnp.exp(-jnp.abs(z)))) * (1.0 / GLA_GATE_NORM)
    parts = _split3(log_a)
    mt = mt_ref[...]
    mo = mo_ref[...]
    cum_ref[...] = sum(jnp.dot(mt, p, preferred_element_type=F32) for p in parts)
    last_ref[...] = sum(jnp.dot(mo, p, preferred_element_type=F32) for p in parts)

    rows_i = lax.broadcasted_iota(jnp.int32, (GLA_CHUNK, GLA_CHUNK), 0)
    cols_i = lax.broadcasted_iota(jnp.int32, (GLA_CHUNK, GLA_CHUNK), 1)
    tri = rows_i >= cols_i
    scale = GLA_DK ** -0.5
    nt_dims = (((1,), (1,)), ((), ()))
    tn_dims = (((0,), (0,)), ((), ()))

    n_chunks = tc // GLA_CHUNK
    cells = [(n, hd) for n in range(n_chunks) for hd in range(GLA_HEADS)]
    rows_of = lambda n: slice(n * GLA_CHUNK, (n + 1) * GLA_CHUNK)
    kc_of = lambda hd: slice(hd * GLA_DK, (hd + 1) * GLA_DK)
    vc_of = lambda hd: slice(hd * GLA_DV, (hd + 1) * GLA_DV)

    cum = cum_ref[...]
    last = last_ref[...]
    kf = k_ref[...].astype(F32)
    qd_ref[...] = (q_ref[...].astype(F32) * scale * jnp.exp(cum)).astype(BF16)
    ki_ref[...] = (kf * jnp.exp(-cum)).astype(BF16)
    kt_ref[...] = (kf * jnp.exp(last - cum)).astype(BF16)

    sc = {}
    for n, hd in cells:
        s = lax.dot_general(qd_ref[rows_of(n), kc_of(hd)], ki_ref[rows_of(n), kc_of(hd)], nt_dims,
                            preferred_element_type=F32)
        sc[n, hd] = jnp.where(tri, s, 0.0).astype(BF16)
    for n, hd in cells:
        u_ref[n, hd] = lax.dot_general(v_ref[rows_of(n), vc_of(hd)], kt_ref[rows_of(n), kc_of(hd)],
                                       tn_dims, preferred_element_type=F32)
    for hd in range(GLA_HEADS):
        st = st_ref[hd]
        for n in range(n_chunks):
            sb_ref[n, hd] = st.astype(BF16)
            dl = jnp.exp(last_ref[n * GLA_CHUNK:n * GLA_CHUNK + 1, kc_of(hd)])
            st = dl * st + u_ref[n, hd]
        st_ref[hd] = st
    for n, hd in cells:
        vv = v_ref[rows_of(n), vc_of(hd)]
        o = jnp.dot(sc[n, hd], vv, preferred_element_type=F32)
        o = o + lax.dot_general(qd_ref[rows_of(n), kc_of(hd)], sb_ref[n, hd], nt_dims,
                                preferred_element_type=F32)
        on = _rms_rows(o, nw_ref[...])
        rr = r_ref[rows_of(n), vc_of(hd)].astype(F32)
        o_ref[rows_of(n), vc_of(hd)] = (on * (rr * jax.nn.sigmoid(rr))).astype(o_ref.dtype)


def _gla(proj, alr, wg, bg, nw, bsz, seq, tc):
    nt = seq // tc
    hk = GLA_HEADS * GLA_DK
    hv = GLA_HEADS * GLA_DV
    pos = np.arange(tc)
    same = (pos[:, None] // GLA_CHUNK) == (pos[None, :] // GLA_CHUNK)
    mt = jnp.asarray(same & (pos[:, None] >= pos[None, :]), BF16)
    mo = jnp.asarray(same, BF16)
    kern = functools.partial(_gla_kernel, tc=tc)
    const2 = lambda b, i: (0, 0)
    row = lambda col: (lambda b, i: (b * nt + i, col))
    return pl.pallas_call(
        kern,
        grid=(bsz, nt),
        in_specs=[
            pl.BlockSpec((tc, hk), row(COL_QG // hk)),
            pl.BlockSpec((tc, hk), row(COL_KG // hk)),
            pl.BlockSpec((tc, hv), row(COL_VG // hv)),
            pl.BlockSpec((tc, hv), row(COL_RG // hv)),
            pl.BlockSpec((tc, LANES), row(0)),
            pl.BlockSpec((LANES, hk), const2),
            pl.BlockSpec((1, hk), const2),
            pl.BlockSpec((1, GLA_DV), const2),
            pl.BlockSpec((tc, tc), const2),
            pl.BlockSpec((tc, tc), const2),
        ],
        out_specs=pl.BlockSpec((tc, hv), row(0)),
        out_shape=jax.ShapeDtypeStruct((bsz * seq, hv), BF16),
        scratch_shapes=[
            pltpu.VMEM((tc, hk), F32),
            pltpu.VMEM((tc, hk), F32),
            pltpu.VMEM((GLA_HEADS, GLA_DV, GLA_DK), F32),
            pltpu.VMEM((tc, hk), BF16),
            pltpu.VMEM((tc, hk), BF16),
            pltpu.VMEM((tc, hk), BF16),
            pltpu.VMEM((tc // GLA_CHUNK, GLA_HEADS, GLA_DV, GLA_DK), F32),
            pltpu.VMEM((tc // GLA_CHUNK, GLA_HEADS, GLA_DV, GLA_DK), BF16),
        ],
        compiler_params=_cparams(2),
        name="gla",
    )(proj, proj, proj, proj, alr, wg, bg, nw, mt, mo)


ATTN_ROW_CHUNK = 16


def _attn_kernel(first_ref, q_ref, k_ref, v_ref, pos_ref, one_ref, lq1_ref, lk1_ref, lq2_ref, lk2_ref,
                 nw_ref, o_ref, qa_ref, s_ref, p_ref, acc_ref, m_ref, l_ref, al_ref,
                 *, tq, nq, lambda_init):
    head = pl.program_id(1)
    qi = pl.program_id(2)
    slope2 = jnp.float32(ALIBI_SLOPES[-1] * LOG2E)
    for hd in range(DIFF_HEADS - 1):
        slope2 = jnp.where(head == hd, jnp.float32(ALIBI_SLOPES[hd] * LOG2E), slope2)
    nt_dims = (((1,), (1,)), ((), ()))
    n_lane_chunks = tq // LANES

    for m in range(2):
        qa_ref[m] = jnp.concatenate([q_ref[:, m * DIFF_DK:(m + 1) * DIFF_DK], one_ref[...]], axis=1)
    acc_ref[...] = jnp.zeros_like(acc_ref)
    l_ref[...] = jnp.zeros_like(l_ref)
    m_ref[...] = jnp.full(m_ref.shape, -jnp.inf, F32)
    pos = pos_ref[0]

    def tile_rows(j):
        return pl.ds(pl.multiple_of(j * tq, tq), tq)

    def tile_bias(j):
        return -slope2 * ((qi - j) * tq).astype(F32)

    def scores(m, j, masked):
        k_aug = jnp.concatenate([k_ref[tile_rows(j), m * DIFF_DK:(m + 1) * DIFF_DK], pos], axis=1)
        s = lax.dot_general(qa_ref[m], k_aug, nt_dims, preferred_element_type=F32)
        if masked:
            keep = (lax.broadcasted_iota(jnp.int32, (tq, tq), 0)
                    >= lax.broadcasted_iota(jnp.int32, (tq, tq), 1))
            s = jnp.where(keep, s, -jnp.inf)
        s_ref[m] = s
        mx = jnp.broadcast_to(jnp.max(s, axis=-1, keepdims=True), (tq, LANES))
        m_prev = m_ref[m]
        m_new = jnp.maximum(m_prev, mx + tile_bias(j))
        al_ref[m] = jnp.exp2(m_prev - m_new)
        m_ref[m] = m_new

    def accumulate(m, j):
        tb = tile_bias(j)
        for c in range(tq // ATTN_ROW_CHUNK):
            rr = pl.ds(c * ATTN_ROW_CHUNK, ATTN_ROW_CHUNK)
            sh = m_ref[m, rr, :] - tb
            for lc in range(n_lane_chunks):
                lanes = slice(lc * LANES, (lc + 1) * LANES)
                p_ref[m, rr, lanes] = jnp.exp2(s_ref[m, rr, lanes] - sh).astype(BF16)
        for c in range(tq // ATTN_ROW_CHUNK):
            rr = pl.ds(c * ATTN_ROW_CHUNK, ATTN_ROW_CHUNK)
            part = None
            for lc in range(n_lane_chunks):
                x = p_ref[m, rr, lc * LANES:(lc + 1) * LANES].astype(F32)
                part = x if part is None else part + x
            l_ref[m, rr, :] = al_ref[m, rr, :] * l_ref[m, rr, :] + part
        pv = jnp.dot(p_ref[m], v_ref[tile_rows(j), :], preferred_element_type=F32)
        al = al_ref[m]
        for hv in range(DIFF_DV // LANES):
            lanes = slice(hv * LANES, (hv + 1) * LANES)
            acc_ref[m, :, lanes] = al * acc_ref[m, :, lanes] + pv[:, lanes]

    def step(j, mask_next):
        accumulate(0, j)
        scores(1, j, False)
        accumulate(1, j)
        scores(0, j + 1, mask_next)

    def diagonal_tile():
        accumulate(0, qi)
        scores(1, qi, True)
        accumulate(1, qi)

    @pl.when(qi == 0)
    def _():
        scores(0, 0, True)
        diagonal_tile()

    first = first_ref[(pl.program_id(0) * DIFF_HEADS + head) * nq + qi]

    @pl.when(qi > 0)
    def _():
        scores(0, first, False)

    def body(j, carry):
        step(j, False)
        return carry

    lax.fori_loop(first, qi - 1, body, 0)

    @pl.when(qi > 0)
    def _():
        step(qi - 1, True)
        diagonal_tile()

    lam = (jnp.exp(jnp.sum(lq1_ref[...] * lk1_ref[...], axis=-1, keepdims=True))
           - jnp.exp(jnp.sum(lq2_ref[...] * lk2_ref[...], axis=-1, keepdims=True)) + lambda_init)
    l1 = jnp.sum(l_ref[0], axis=-1, keepdims=True)
    l2 = jnp.sum(l_ref[1], axis=-1, keepdims=True)
    o = acc_ref[0] / l1 - lam * (acc_ref[1] / l2)
    o = _rms_rows(o, nw_ref[...]) * (1.0 - lambda_init)
    o_ref[...] = o.astype(o_ref.dtype)


def _attn_pos_columns(tq):
    slopes2 = np.asarray([np.float32(s * LOG2E) for s in ALIBI_SLOPES], np.float32)
    rest = np.arange(tq, dtype=np.float32)[None, :] * slopes2[:, None]
    pos = np.zeros((DIFF_HEADS, tq, LANES), BF16)
    one = np.zeros((tq, LANES), BF16)
    for col in range(3):
        part = rest.astype(BF16)
        pos[:, :, col] = part
        one[:, col] = 1.0
        rest = rest - part.astype(np.float32)
    return jnp.asarray(pos), jnp.asarray(one)


ATTN_UNDERFLOW_LOG2 = 200.0


def _attn_first_tiles(proj, bsz, seq, tq):
    nq = seq // tq

    def row_norms(col):
        v = proj[:, col:col + DIFF_HEADS * 2 * DIFF_DK].astype(F32)
        v = v.reshape(bsz, seq, DIFF_HEADS * 2, DIFF_DK)
        n = jnp.sqrt(jnp.sum(v * v, axis=-1)).reshape(bsz, seq, DIFF_HEADS, 2)
        return jnp.max(n, axis=-1)

    qn = jnp.max(row_norms(COL_QD).reshape(bsz, nq, tq, DIFF_HEADS), axis=2)
    kn = jnp.max(row_norms(COL_KD), axis=1)
    slope_tile = jnp.asarray([np.float32(s * LOG2E) for s in ALIBI_SLOPES], F32) * tq
    reach = 1.0 + (2.0 * 1.01 * qn * kn[:, None, :] + ATTN_UNDERFLOW_LOG2) / slope_tile
    reach = jnp.where(jnp.isfinite(reach), jnp.minimum(reach, float(nq)), float(nq))
    keep = jnp.floor(reach).astype(jnp.int32)
    first = jnp.maximum(jnp.arange(nq, dtype=jnp.int32)[None, :, None] - keep, 0)
    return jnp.transpose(first, (0, 2, 1)).reshape(-1)


def _attn(proj, lq1, lk1, lq2, lk2, nw, bsz, seq, tq, lambda_init):
    nq = seq // tq
    w2 = 2 * DIFF_DK
    kern = functools.partial(_attn_kernel, tq=tq, nq=nq, lambda_init=lambda_init)
    const2 = lambda b, h, i: (0, 0)
    pos, one = _attn_pos_columns(tq)
    first = _attn_first_tiles(proj, bsz, seq, tq)
    stat = pltpu.VMEM((2, tq, LANES), F32)
    return pl.pallas_call(
        kern,
        grid=(bsz, DIFF_HEADS, nq),
        in_specs=[
            pl.BlockSpec(memory_space=pltpu.SMEM),
            pl.BlockSpec((tq, w2), lambda b, h, i: (b * nq + i, COL_QD // w2 + h)),
            pl.BlockSpec((seq, w2), lambda b, h, i: (b, COL_KD // w2 + h),
                         pipeline_mode=pl.Buffered(1)),
            pl.BlockSpec((seq, DIFF_DV), lambda b, h, i: (b, COL_VD // DIFF_DV + h),
                         pipeline_mode=pl.Buffered(1)),
            pl.BlockSpec((1, tq, LANES), lambda b, h, i: (h, 0, 0)),
            pl.BlockSpec((tq, LANES), const2),
            pl.BlockSpec((1, DIFF_DK), const2),
            pl.BlockSpec((1, DIFF_DK), const2),
            pl.BlockSpec((1, DIFF_DK), const2),
            pl.BlockSpec((1, DIFF_DK), const2),
            pl.BlockSpec((1, DIFF_DV), const2),
        ],
        out_specs=pl.BlockSpec((tq, DIFF_DV), lambda b, h, i: (b * nq + i, h)),
        out_shape=jax.ShapeDtypeStruct((bsz * seq, DIFF_HEADS * DIFF_DV), BF16),
        scratch_shapes=[
            pltpu.VMEM((2, tq, 2 * DIFF_DK), BF16),
            pltpu.VMEM((2, tq, tq), F32),
            pltpu.VMEM((2, tq, tq), BF16),
            pltpu.VMEM((2, tq, DIFF_DV), F32),
            stat, stat, stat,
        ],
        compiler_params=_cparams(3),
        name="diffattn",
    )(first, proj, proj, proj, pos, one, lq1, lk1, lq2, lk2, nw)


def _merge_kernel(gy_ref, ob_ref, oc_ref, g0_ref, g1_ref, g2_ref, wgv_ref, wgg_ref, wb_ref, wc_ref,
                  wo_ref, h_ref, o_ref, acc_ref):
    j = pl.program_id(1)

    @pl.when(j == 0)
    def _():
        acc_ref[...] = jnp.zeros_like(acc_ref)

    gy = gy_ref[...]
    val = jnp.dot(gy, wgv_ref[...], preferred_element_type=F32)
    gate = jnp.dot(gy, wgg_ref[...], preferred_element_type=F32)
    ya = val * jax.nn.sigmoid(gate)
    yb = jnp.dot(ob_ref[...], wb_ref[...], preferred_element_type=F32)
    yc = jnp.dot(oc_ref[...], wc_ref[...], preferred_element_type=F32)
    merged = (jax.nn.sigmoid(g0_ref[...].astype(F32)) * ya
              + jax.nn.sigmoid(g1_ref[...].astype(F32)) * yb
              + jax.nn.sigmoid(g2_ref[...].astype(F32)) * yc)
    acc_ref[...] += jnp.dot(merged.astype(BF16), wo_ref[...], preferred_element_type=F32)

    @pl.when(j == pl.num_programs(1) - 1)
    def _():
        o_ref[...] = h_ref[...] + acc_ref[...]


def _merge(gy, ob, oc, proj, w_glu, wbr_g, wbr_d, w_out, h, tm, tn):
    t = h.shape[0]
    nj = D_MODEL // tn
    gt0 = COL_GT // tn
    left = lambda i, j: (i, 0)
    return pl.pallas_call(
        _merge_kernel,
        grid=(t // tm, nj),
        in_specs=[
            pl.BlockSpec((tm, D_SSM), left),
            pl.BlockSpec((tm, GLA_HEADS * GLA_DV), left),
            pl.BlockSpec((tm, DIFF_HEADS * DIFF_DV), left),
            pl.BlockSpec((tm, tn), lambda i, j: (i, gt0 + j)),
            pl.BlockSpec((tm, tn), lambda i, j: (i, gt0 + nj + j)),
            pl.BlockSpec((tm, tn), lambda i, j: (i, gt0 + 2 * nj + j)),
            pl.BlockSpec((D_SSM, tn), lambda i, j: (0, j)),
            pl.BlockSpec((D_SSM, tn), lambda i, j: (0, nj + j)),
            pl.BlockSpec((GLA_HEADS * GLA_DV, tn), lambda i, j: (0, j)),
            pl.BlockSpec((DIFF_HEADS * DIFF_DV, tn), lambda i, j: (0, j)),
            pl.BlockSpec((tn, D_MODEL), lambda i, j: (j, 0)),
            pl.BlockSpec((tm, D_MODEL), left),
        ],
        out_specs=pl.BlockSpec((tm, D_MODEL), left),
        out_shape=jax.ShapeDtypeStruct((t, D_MODEL), F32),
        scratch_shapes=[pltpu.VMEM((tm, D_MODEL), F32)],
        compiler_params=_cparams(2),
        name="merge",
    )(gy, ob, oc, proj, proj, proj, w_glu, w_glu, wbr_g, wbr_d, w_out, h)


def _ffn_kernel(h_ref, halo_ref, nw_ref, wv_ref, wg_ref, cwv_ref, cwg_ref, cbv_ref, cbg_ref, wd_ref,
                fnw_ref, o_ref, hn_ref, *, tm, tiles_per_seq, final_norm):
    i = pl.program_id(0)
    j = pl.program_id(1)

    @pl.when(j == 0)
    def _():
        keep = (i % tiles_per_seq != 0).astype(F32)
        hn_ref[0:HALO, :] = (_rms_rows(halo_ref[...], nw_ref[...]) * keep).astype(BF16)
        hn_ref[HALO:, :] = _rms_rows(h_ref[...], nw_ref[...]).astype(BF16)
        o_ref[...] = h_ref[...]

    hn = hn_ref[...]

    def conv(u, cw_ref, cb_ref, cols):
        acc = cb_ref[:, cols]
        for tap in range(CONV_W):
            lo = HALO - (CONV_W - 1) + tap
            acc = acc + u[lo:lo + tm, :] * cw_ref[tap:tap + 1, cols]
        return acc

    subs = [slice(s * FFN_SUBTILE, (s + 1) * FFN_SUBTILE) for s in range(FFN_TILE // FFN_SUBTILE)]
    ups = [(jnp.dot(hn, wv_ref[:, cols], preferred_element_type=F32),
            jnp.dot(hn, wg_ref[:, cols], preferred_element_type=F32)) for cols in subs]
    down = None
    for cols, (uv, ug) in zip(subs, ups):
        val = conv(uv, cwv_ref, cbv_ref, cols)
        gate = conv(ug, cwg_ref, cbg_ref, cols)
        act = (gate * jax.nn.sigmoid(gate) * val).astype(BF16)
        part = jnp.dot(act, wd_ref[cols, :], preferred_element_type=F32)
        down = part if down is None else down + part
    o_ref[...] += down

    if final_norm:
        @pl.when(j == pl.num_programs(1) - 1)
        def _():
            o_ref[...] = _rms_rows(o_ref[...], fnw_ref[...])


def _ffn(h, nw, wv, wg, cwv, cwg, cbv, cbg, wd, fnw, seq, tm, final_norm):
    t = h.shape[0]
    tf = FFN_TILE
    kern = functools.partial(_ffn_kernel, tm=tm, tiles_per_seq=seq // tm, final_norm=final_norm)
    halo_blocks = tm // HALO
    const2 = lambda i, j: (0, 0)
    col = lambda i, j: (0, j)
    return pl.pallas_call(
        kern,
        grid=(t // tm, D_FF_PAD // tf),
        in_specs=[
            pl.BlockSpec((tm, D_MODEL), lambda i, j: (i, 0), pipeline_mode=pl.Buffered(1)),
            pl.BlockSpec((HALO, D_MODEL), lambda i, j: (jnp.maximum(i * halo_blocks - 1, 0), 0)),
            pl.BlockSpec((1, D_MODEL), const2),
            pl.BlockSpec((D_MODEL, tf), col),
            pl.BlockSpec((D_MODEL, tf), col),
            pl.BlockSpec((CONV_W, tf), col),
            pl.BlockSpec((CONV_W, tf), col),
            pl.BlockSpec((1, tf), col),
            pl.BlockSpec((1, tf), col),
            pl.BlockSpec((tf, D_MODEL), lambda i, j: (j, 0)),
            pl.BlockSpec((1, D_MODEL), const2),
        ],
        out_specs=pl.BlockSpec((tm, D_MODEL), lambda i, j: (i, 0)),
        out_shape=jax.ShapeDtypeStruct((t, D_MODEL), F32),
        scratch_shapes=[pltpu.VMEM((HALO + tm, D_MODEL), BF16)],
        compiler_params=_cparams(2),
        name="ffn",
    )(h, h, nw, wv, wg, cwv, cwg, cbv, cbg, wd, fnw)


def _pad_cols(a, n):
    return jnp.pad(a, ((0, 0), (0, n - a.shape[1])))


def _tile(n, pref):
    return pref if n % pref == 0 else n


def kernel(x, norm_mix_w, w_in, ssm_a_re, ssm_a_im, ssm_log_dt, ssm_b_re, ssm_b_im, ssm_c_re, ssm_c_im, ssm_d, ssm_w_glu, gla_w_gate, gla_b_gate, gla_norm_w, gla_w_br, diff_lam_q1, diff_lam_k1, diff_lam_q2, diff_lam_k2, diff_norm_w, diff_w_br, w_out, norm_ffn_w, ffn_w_up, ffn_conv_w, ffn_conv_b, ffn_w_down, norm_final_w):
    bsz, seq, _ = x.shape
    t = bsz * seq
    h = x.reshape(t, D_MODEL).astype(F32)
    row = lambda v: v.reshape(1, -1).astype(F32)

    for l in range(DEPTH):
        lambda_init = 0.8 - 0.6 * math.exp(-0.3 * l)
        w = w_in[l]
        w_main = jnp.concatenate([w[:, :ALR_ORIG], w[:, ALR_ORIG + GLA_GATE_RANK:]], axis=1).astype(BF16)
        w_alr = _pad_cols(w[:, ALR_ORIG:ALR_ORIG + GLA_GATE_RANK], LANES).astype(BF16)
        proj, alr = _inproj(h, row(norm_mix_w[l]), w_main, w_alr, _tile(t, 1024), 1024)

        bw, cw, are, aim = _s5_weights(ssm_a_re[l], ssm_a_im[l], ssm_log_dt[l], ssm_b_re[l],
                                       ssm_b_im[l], ssm_c_re[l], ssm_c_im[l])
        gy = _s5(proj, bw, cw, are, aim, row(ssm_d[l]), bsz, seq, _tile(seq, 256))

        wg = jnp.pad(gla_w_gate[l], ((0, LANES - GLA_GATE_RANK), (0, 0))).astype(BF16)
        ob = _gla(proj, alr, wg, row(gla_b_gate[l]), row(gla_norm_w[l]), bsz, seq, _tile(seq, 256))

        oc = _attn(proj, row(diff_lam_q1[l]), row(diff_lam_k1[l]), row(diff_lam_q2[l]),
                   row(diff_lam_k2[l]), row(diff_norm_w[l]), bsz, seq, _tile(seq, 1024), lambda_init)

        h = _merge(gy, ob, oc, proj, ssm_w_glu[l].astype(BF16), gla_w_br[l].astype(BF16),
                   diff_w_br[l].astype(BF16), w_out[l].astype(BF16), h, _tile(t, 512), 512)

        wup = ffn_w_up[l]
        cwf = ffn_conv_w[l].astype(F32)
        cbf = ffn_conv_b[l].reshape(1, -1).astype(F32)
        h = _ffn(h, row(norm_ffn_w[l]),
                 _pad_cols(wup[:, :D_FF], D_FF_PAD).astype(BF16),
                 _pad_cols(wup[:, D_FF:], D_FF_PAD).astype(BF16),
                 _pad_cols(cwf[:, :D_FF], D_FF_PAD), _pad_cols(cwf[:, D_FF:], D_FF_PAD),
                 _pad_cols(cbf[:, :D_FF], D_FF_PAD), _pad_cols(cbf[:, D_FF:], D_FF_PAD),
                 jnp.pad(ffn_w_down[l], ((0, D_FF_PAD - D_FF), (0, 0))).astype(BF16),
                 row(norm_final_w), seq, _tile(seq, 1024), l == DEPTH - 1)

    return h.reshape(bsz, seq, D_MODEL).astype(x.dtype)
```

```python
import functools
import math

import jax
import jax.numpy as jnp
import numpy as np
from jax import lax
from jax.experimental import pallas as pl
from jax.experimental.pallas import tpu as pltpu

F32 = jnp.float32
BF16 = jnp.bfloat16

D_MODEL = 2048
DEPTH = 2
D_SSM = 1024
SSM_GROUP = 16
N_SSM_GROUPS = D_SSM // SSM_GROUP
SSM_STATE = 64
GLA_HEADS = 4
GLA_DK = 128
GLA_DV = 256
GLA_GATE_RANK = 16
GLA_GATE_NORM = 16.0
GLA_CHUNK = 32
DIFF_HEADS = 4
DIFF_DK = 128
DIFF_DV = 256
N_BRANCH = 3
D_FF = 5504
CONV_W = 3
EPS = 1e-6
ALIBI_SLOPES = tuple(2.0 ** (-8.0 * (h + 1) / DIFF_HEADS) for h in range(DIFF_HEADS))
LOG2E = math.log2(math.e)

LANES = 128
SUBLANES = 8
BF16_ROWS = 16

COL_U = 0
COL_QG = COL_U + D_SSM
COL_KG = COL_QG + GLA_HEADS * GLA_DK
COL_VG = COL_KG + GLA_HEADS * GLA_DK
COL_RG = COL_VG + GLA_HEADS * GLA_DV
COL_QD = COL_RG + GLA_HEADS * GLA_DV
COL_KD = COL_QD + DIFF_HEADS * 2 * DIFF_DK
COL_VD = COL_KD + DIFF_HEADS * 2 * DIFF_DK
COL_GT = COL_VD + DIFF_HEADS * DIFF_DV
N_MAIN = COL_GT + N_BRANCH * D_MODEL
ALR_ORIG = COL_QD

D_FF_PAD = 5632
FFN_TILE = 512
FFN_SUBTILE = 256
HALO = BF16_ROWS

VMEM_LIMIT = 56 * 1024 * 1024


def _cparams(n_axes, flags=None):
    return pltpu.CompilerParams(
        dimension_semantics=("arbitrary",) * n_axes, vmem_limit_bytes=VMEM_LIMIT, flags=flags)


def _rms_rows(x, w):
    ms = jnp.mean(x * x, axis=-1, keepdims=True)
    return x * lax.rsqrt(ms + EPS) * w


def _inproj_kernel(h_ref, nw_ref, w_ref, walr_ref, cs_ref, o_ref, alr_ref, nrm_ref, hn_ref, *, jq, jk):
    j = pl.program_id(1)

    @pl.when(j == 0)
    def _():
        hn = _rms_rows(h_ref[...], nw_ref[...]).astype(BF16)
        hn_ref[...] = hn
        alr_ref[...] = jnp.dot(hn, walr_ref[...], preferred_element_type=F32).astype(alr_ref.dtype)

    acc = jnp.dot(hn_ref[...], w_ref[...], preferred_element_type=F32)
    out = (acc * cs_ref[...]).astype(o_ref.dtype)
    o_ref[...] = out

    @pl.when((j == jq) | (j == jk))
    def _():
        ob = out.astype(F32)
        sq = ob * ob
        rows = []
        for g in range(sq.shape[1] // DIFF_DK):
            ssum = jnp.sum(sq[:, g * DIFF_DK:(g + 1) * DIFF_DK], axis=-1, keepdims=True)
            rows.append(jnp.broadcast_to(jnp.max(ssum, axis=0, keepdims=True), (1, LANES)))
        nrm_ref[0, 0] = jnp.concatenate(rows, axis=0)


def _inproj(h, nw, w_main, w_alr, tm, tn):
    t = h.shape[0]
    groups = DIFF_HEADS * 2
    assert tn == groups * DIFF_DK and COL_QD % tn == 0 and COL_KD == COL_QD + tn
    jq = COL_QD // tn
    kern = functools.partial(_inproj_kernel, jq=jq, jk=jq + 1)
    cs = jnp.ones((1, N_MAIN), F32).at[:, COL_QD:COL_KD].set(DIFF_DK ** -0.5 * LOG2E)
    return pl.pallas_call(
        kern,
        grid=(t // tm, N_MAIN // tn),
        in_specs=[
            pl.BlockSpec((tm, D_MODEL), lambda i, j: (i, 0)),
            pl.BlockSpec((1, D_MODEL), lambda i, j: (0, 0)),
            pl.BlockSpec((D_MODEL, tn), lambda i, j: (0, j)),
            pl.BlockSpec((D_MODEL, LANES), lambda i, j: (0, 0)),
            pl.BlockSpec((1, tn), lambda i, j: (0, j)),
        ],
        out_specs=[
            pl.BlockSpec((tm, tn), lambda i, j: (i, j)),
            pl.BlockSpec((tm, LANES), lambda i, j: (i, 0)),
            pl.BlockSpec((1, 1, groups, LANES), lambda i, j: (i, jnp.clip(j - jq, 0, 1), 0, 0)),
        ],
        out_shape=[
            jax.ShapeDtypeStruct((t, N_MAIN), BF16),
            jax.ShapeDtypeStruct((t, LANES), BF16),
            jax.ShapeDtypeStruct((t // tm, 2, groups, LANES), F32),
        ],
        scratch_shapes=[pltpu.VMEM((tm, D_MODEL), BF16)],
        compiler_params=_cparams(2),
        name="inproj",
    )(h, nw, w_main, w_alr, cs)


S5_CHUNKS = D_SSM // LANES
S5_SLABS = N_SSM_GROUPS * SSM_STATE // LANES
S5_SLABS_PER_CHUNK = S5_SLABS // S5_CHUNKS
S5_VREGS = S5_SLABS // SUBLANES


def _s5_kernel(u_ref, bw_ref, cw_ref, are_ref, aim_ref, d_ref, o_ref, xs_ref, st_ref, *, nb, tc, pitch):
    @pl.when(pl.program_id(0) == 0)
    def _():
        st_ref[...] = jnp.zeros_like(st_ref)

    half = S5_SLABS_PER_CHUNK * LANES
    a_re = [are_ref[k * SUBLANES:(k + 1) * SUBLANES, :] for k in range(S5_VREGS)]
    a_im = [aim_ref[k * SUBLANES:(k + 1) * SUBLANES, :] for k in range(S5_VREGS)]

    def slab(b, r):
        return (b * 2 * S5_SLABS + r) * pitch

    def drive(b):
        for c in range(S5_CHUNKS):
            x = jnp.dot(u_ref[b, :, c * LANES:(c + 1) * LANES], bw_ref[c], preferred_element_type=F32)
            for q in range(S5_SLABS_PER_CHUNK):
                r = c * S5_SLABS_PER_CHUNK + q
                xs_ref[pl.ds(slab(b, r), tc), :] = x[:, q * LANES:(q + 1) * LANES]
                xs_ref[pl.ds(slab(b, S5_SLABS + r), tc), :] = x[:, half + q * LANES:half + (q + 1) * LANES]

    def scan(b):
        st0 = b * 2 * S5_SLABS
        state = []
        for k in range(S5_VREGS):
            state.append([st_ref[st0 + k * SUBLANES:st0 + (k + 1) * SUBLANES, :],
                          st_ref[st0 + S5_SLABS + k * SUBLANES:st0 + S5_SLABS + (k + 1) * SUBLANES, :]])
        for t in range(tc):
            for k in range(S5_VREGS):
                re, im = state[k]
                idx_re = pl.ds(slab(b, k * SUBLANES) + t, SUBLANES, stride=pitch)
                idx_im = pl.ds(slab(b, S5_SLABS + k * SUBLANES) + t, SUBLANES, stride=pitch)
                nre = a_re[k] * re - a_im[k] * im + xs_ref[idx_re, :]
                nim = a_re[k] * im + a_im[k] * re + xs_ref[idx_im, :]
                xs_ref[idx_re, :] = nre
                xs_ref[idx_im, :] = nim
                state[k] = [nre, nim]
        for k in range(S5_VREGS):
            st_ref[st0 + k * SUBLANES:st0 + (k + 1) * SUBLANES, :] = state[k][0]
            st_ref[st0 + S5_SLABS + k * SUBLANES:st0 + S5_SLABS + (k + 1) * SUBLANES, :] = state[k][1]

    def readout(b):
        for c in range(S5_CHUNKS):
            parts = []
            for base in (0, S5_SLABS):
                for q in range(S5_SLABS_PER_CHUNK):
                    r = base + c * S5_SLABS_PER_CHUNK + q
                    parts.append(xs_ref[pl.ds(slab(b, r), tc), :].astype(BF16))
            hcat = jnp.concatenate(parts, axis=1)
            y = jnp.dot(hcat, cw_ref[c], preferred_element_type=F32)
            lanes = slice(c * LANES, (c + 1) * LANES)
            y = y + d_ref[:, lanes] * u_ref[b, :, lanes].astype(F32)
            o_ref[b, :, lanes] = jax.nn.gelu(y).astype(o_ref.dtype)

    for b in range(nb):
        drive(b)
    for b in range(nb):
        scan(b)
        readout(b)


def _s5_weights(a_re, a_im, log_dt, b_re, b_im, c_re, c_im):
    lam_r = jnp.minimum(a_re.astype(F32), -1e-4)
    lam_i = a_im.astype(F32)
    dt = jnp.exp(log_dt.astype(F32))[:, None]
    mag = jnp.exp(dt * lam_r)
    ab_r = mag * jnp.cos(dt * lam_i)
    ab_i = mag * jnp.sin(dt * lam_i)
    den = lam_r * lam_r + lam_i * lam_i
    f_r = ((ab_r - 1.0) * lam_r + ab_i * lam_i) / den
    f_i = (ab_i * lam_r - (ab_r - 1.0) * lam_i) / den
    br = b_re.astype(F32)
    bi = b_im.astype(F32)
    bb_r = f_r[..., None] * br - f_i[..., None] * bi
    bb_i = f_r[..., None] * bi + f_i[..., None] * br
    gpc = LANES // SSM_GROUP
    eye = jnp.eye(gpc, dtype=F32)

    def drive(bb):
        bb = bb.reshape(S5_CHUNKS, gpc, SSM_STATE, SSM_GROUP)
        m = jnp.einsum('cgpk,gh->cgkhp', bb, eye)
        return m.reshape(S5_CHUNKS, LANES, gpc * SSM_STATE)

    def readout(cc):
        cc = cc.astype(F32).reshape(S5_CHUNKS, gpc, SSM_GROUP, SSM_STATE)
        m = jnp.einsum('cgkp,gh->cgphk', cc, eye)
        return m.reshape(S5_CHUNKS, gpc * SSM_STATE, LANES)

    bw = jnp.concatenate([drive(bb_r), drive(bb_i)], axis=2).astype(BF16)
    cw = jnp.concatenate([readout(c_re), -readout(c_im)], axis=1).astype(BF16)
    return bw, cw, ab_r.reshape(S5_SLABS, LANES), ab_i.reshape(S5_SLABS, LANES)


def _s5(proj, bw, cw, are, aim, d_skip, bsz, seq, tc):
    nt = seq // tc
    pitch = tc + SUBLANES
    kern = functools.partial(_s5_kernel, nb=bsz, tc=tc, pitch=pitch)
    const3 = lambda i: (0, 0, 0)
    const2 = lambda i: (0, 0)
    out = pl.pallas_call(
        kern,
        grid=(nt,),
        in_specs=[
            pl.BlockSpec((bsz, tc, D_SSM), lambda i: (0, i, COL_U // D_SSM)),
            pl.BlockSpec(bw.shape, const3),
            pl.BlockSpec(cw.shape, const3),
            pl.BlockSpec(are.shape, const2),
            pl.BlockSpec(aim.shape, const2),
            pl.BlockSpec((1, D_SSM), const2),
        ],
        out_specs=pl.BlockSpec((bsz, tc, D_SSM), lambda i: (0, i, 0)),
        out_shape=jax.ShapeDtypeStruct((bsz, seq, D_SSM), BF16),
        scratch_shapes=[
            pltpu.VMEM((bsz * 2 * S5_SLABS * pitch, LANES), F32),
            pltpu.VMEM((bsz * 2 * S5_SLABS, LANES), F32),
        ],
        compiler_params=_cparams(1),
        name="s5",
    )(proj.reshape(bsz, seq, N_MAIN), bw, cw, are, aim, d_skip)
    return out.reshape(bsz * seq, D_SSM)


def _split3(x):
    hi = x.astype(BF16)
    r1 = x - hi.astype(F32)
    mid = r1.astype(BF16)
    lo = (r1 - mid.astype(F32)).astype(BF16)
    return hi, mid, lo


def _gla_kernel(q_ref, k_ref, v_ref, r_ref, alr_ref, wg_ref, bg_ref, nw_ref, mt_ref, mo_ref,
                o_ref, cum_ref, last_ref, st_ref, qd_ref, ki_ref, kt_ref, u_ref, sb_ref, *, tc):
    @pl.when(pl.program_id(1) == 0)
    def _():
        st_ref[...] = jnp.zeros_like(st_ref)

    z = jnp.dot(alr_ref[...], wg_ref[...], preferred_element_type=F32) + bg_ref[...]
    log_a = (jnp.minimum(z, 0.0) - jnp.log1p(jnp.exp(-jnp.abs(z)))) * (1.0 / GLA_GATE_NORM)
    parts = _split3(log_a)
    mt = mt_ref[...]
    mo = mo_ref[...]
    cum_ref[...] = sum(jnp.dot(mt, p, preferred_element_type=F32) for p in parts)
    last_ref[...] = sum(jnp.dot(mo, p, preferred_element_type=F32) for p in parts)

    rows_i = lax.broadcasted_iota(jnp.int32, (GLA_CHUNK, GLA_CHUNK), 0)
    cols_i = lax.broadcasted_iota(jnp.int32, (GLA_CHUNK, GLA_CHUNK), 1)
    tri = rows_i >= cols_i
    scale = GLA_DK ** -0.5
    nt_dims = (((1,), (1,)), ((), ()))
    tn_dims = (((0,), (0,)), ((), ()))

    n_chunks = tc // GLA_CHUNK
    cells = [(n, hd) for n in range(n_chunks) for hd in range(GLA_HEADS)]
    rows_of = lambda n: slice(n * GLA_CHUNK, (n + 1) * GLA_CHUNK)
    kc_of = lambda hd: slice(hd * GLA_DK, (hd + 1) * GLA_DK)
    vc_of = lambda hd: slice(hd * GLA_DV, (hd + 1) * GLA_DV)

    cum = cum_ref[...]
    last = last_ref[...]
    kf = k_ref[...].astype(F32)
    qd_ref[...] = (q_ref[...].astype(F32) * scale * jnp.exp(cum)).astype(BF16)
    ki_ref[...] = (kf * jnp.exp(-cum)).astype(BF16)
    kt_ref[...] = (kf * jnp.exp(last - cum)).astype(BF16)

    sc = {}
    for n, hd in cells:
        s = lax.dot_general(qd_ref[rows_of(n), kc_of(hd)], ki_ref[rows_of(n), kc_of(hd)], nt_dims,
                            preferred_element_type=F32)
        sc[n, hd] = jnp.where(tri, s, 0.0).astype(BF16)
    for n, hd in cells:
        u_ref[n, hd] = lax.dot_general(v_ref[rows_of(n), vc_of(hd)], kt_ref[rows_of(n), kc_of(hd)],
                                       tn_dims, preferred_element_type=F32)
    for hd in range(GLA_HEADS):
        st = st_ref[hd]
        for n in range(n_chunks):
            sb_ref[n, hd] = st.astype(BF16)
            dl = jnp.exp(last_ref[n * GLA_CHUNK:n * GLA_CHUNK + 1, kc_of(hd)])
            st = dl * st + u_ref[n, hd]
        st_ref[hd] = st
    for n, hd in cells:
        vv = v_ref[rows_of(n), vc_of(hd)]
        o = jnp.dot(sc[n, hd], vv, preferred_element_type=F32)
        o = o + lax.dot_general(qd_ref[rows_of(n), kc_of(hd)], sb_ref[n, hd], nt_dims,
                                preferred_element_type=F32)
        on = _rms_rows(o, nw_ref[...])
        rr = r_ref[rows_of(n), vc_of(hd)].astype(F32)
        o_ref[rows_of(n), vc_of(hd)] = (on * (rr * jax.nn.sigmoid(rr))).astype(o_ref.dtype)


def _gla(proj, alr, wg, bg, nw, bsz, seq, tc):
    nt = seq // tc
    hk = GLA_HEADS * GLA_DK
    hv = GLA_HEADS * GLA_DV
    pos = np.arange(tc)
    same = (pos[:, None] // GLA_CHUNK) == (pos[None, :] // GLA_CHUNK)
    mt = jnp.asarray(same & (pos[:, None] >= pos[None, :]), BF16)
    mo = jnp.asarray(same, BF16)
    kern = functools.partial(_gla_kernel, tc=tc)
    const2 = lambda b, i: (0, 0)
    row = lambda col: (lambda b, i: (b * nt + i, col))
    return pl.pallas_call(
        kern,
        grid=(bsz, nt),
        in_specs=[
            pl.BlockSpec((tc, hk), row(COL_QG // hk)),
            pl.BlockSpec((tc, hk), row(COL_KG // hk)),
            pl.BlockSpec((tc, hv), row(COL_VG // hv)),
            pl.BlockSpec((tc, hv), row(COL_RG // hv)),
            pl.BlockSpec((tc, LANES), row(0)),
            pl.BlockSpec((LANES, hk), const2),
            pl.BlockSpec((1, hk), const2),
            pl.BlockSpec((1, GLA_DV), const2),
            pl.BlockSpec((tc, tc), const2),
            pl.BlockSpec((tc, tc), const2),
        ],
        out_specs=pl.BlockSpec((tc, hv), row(0)),
        out_shape=jax.ShapeDtypeStruct((bsz * seq, hv), BF16),
        scratch_shapes=[
            pltpu.VMEM((tc, hk), F32),
            pltpu.VMEM((tc, hk), F32),
            pltpu.VMEM((GLA_HEADS, GLA_DV, GLA_DK), F32),
            pltpu.VMEM((tc, hk), BF16),
            pltpu.VMEM((tc, hk), BF16),
            pltpu.VMEM((tc, hk), BF16),
            pltpu.VMEM((tc // GLA_CHUNK, GLA_HEADS, GLA_DV, GLA_DK), F32),
            pltpu.VMEM((tc // GLA_CHUNK, GLA_HEADS, GLA_DV, GLA_DK), BF16),
        ],
        compiler_params=_cparams(2),
        name="gla",
    )(proj, proj, proj, proj, alr, wg, bg, nw, mt, mo)


ATTN_ROW_CHUNK = 16


def _attn_kernel(first_ref, q_ref, k_ref, v_ref, pos_ref, one_ref, lq1_ref, lk1_ref, lq2_ref, lk2_ref,
                 nw_ref, o_ref, qa_ref, s_ref, p_ref, acc_ref, m_ref, l_ref, al_ref,
                 *, tq, nq, lambda_init):
    head = pl.program_id(1)
    qi = pl.program_id(2)
    slope2 = jnp.float32(ALIBI_SLOPES[-1] * LOG2E)
    for hd in range(DIFF_HEADS - 1):
        slope2 = jnp.where(head == hd, jnp.float32(ALIBI_SLOPES[hd] * LOG2E), slope2)
    nt_dims = (((1,), (1,)), ((), ()))
    n_lane_chunks = tq // LANES

    for m in range(2):
        qa_ref[m] = jnp.concatenate([q_ref[:, m * DIFF_DK:(m + 1) * DIFF_DK], one_ref[...]], axis=1)
    acc_ref[...] = jnp.zeros_like(acc_ref)
    l_ref[...] = jnp.zeros_like(l_ref)
    m_ref[...] = jnp.full(m_ref.shape, -jnp.inf, F32)
    pos = pos_ref[0]

    def tile_rows(j):
        return pl.ds(pl.multiple_of(j * tq, tq), tq)

    def tile_bias(j):
        return -slope2 * ((qi - j) * tq).astype(F32)

    def scores(m, j, masked):
        k_aug = jnp.concatenate([k_ref[tile_rows(j), m * DIFF_DK:(m + 1) * DIFF_DK], pos], axis=1)
        s = lax.dot_general(qa_ref[m], k_aug, nt_dims, preferred_element_type=F32)
        if masked:
            keep = (lax.broadcasted_iota(jnp.int32, (tq, tq), 0)
                    >= lax.broadcasted_iota(jnp.int32, (tq, tq), 1))
            s = jnp.where(keep, s, -jnp.inf)
        s_ref[m] = s
        mx = jnp.broadcast_to(jnp.max(s, axis=-1, keepdims=True), (tq, LANES))
        m_prev = m_ref[m]
        m_new = jnp.maximum(m_prev, mx + tile_bias(j))
        al_ref[m] = jnp.exp2(m_prev - m_new)
        m_ref[m] = m_new

    def accumulate(m, j):
        tb = tile_bias(j)
        for c in range(tq // ATTN_ROW_CHUNK):
            rr = pl.ds(c * ATTN_ROW_CHUNK, ATTN_ROW_CHUNK)
            sh = m_ref[m, rr, :] - tb
            for lc in range(n_lane_chunks):
                lanes = slice(lc * LANES, (lc + 1) * LANES)
                p_ref[m, rr, lanes] = jnp.exp2(s_ref[m, rr, lanes] - sh).astype(BF16)
        for c in range(tq // ATTN_ROW_CHUNK):
            rr = pl.ds(c * ATTN_ROW_CHUNK, ATTN_ROW_CHUNK)
            part = None
            for lc in range(n_lane_chunks):
                x = p_ref[m, rr, lc * LANES:(lc + 1) * LANES].astype(F32)
                part = x if part is None else part + x
            l_ref[m, rr, :] = al_ref[m, rr, :] * l_ref[m, rr, :] + part
        pv = jnp.dot(p_ref[m], v_ref[tile_rows(j), :], preferred_element_type=F32)
        al = al_ref[m]
        for hv in range(DIFF_DV // LANES):
            lanes = slice(hv * LANES, (hv + 1) * LANES)
            acc_ref[m, :, lanes] = al * acc_ref[m, :, lanes] + pv[:, lanes]

    def step(j, mask_next):
        accumulate(0, j)
        scores(1, j, False)
        accumulate(1, j)
        scores(0, j + 1, mask_next)

    def diagonal_tile():
        accumulate(0, qi)
        scores(1, qi, True)
        accumulate(1, qi)

    @pl.when(qi == 0)
    def _():
        scores(0, 0, True)
        diagonal_tile()

    first = first_ref[(pl.program_id(0) * DIFF_HEADS + head) * nq + qi]

    @pl.when(qi > 0)
    def _():
        scores(0, first, False)

    def body(j, carry):
        step(j, False)
        return carry

    lax.fori_loop(first, qi - 1, body, 0)

    @pl.when(qi > 0)
    def _():
        step(qi - 1, True)
        diagonal_tile()

    lam = (jnp.exp(jnp.sum(lq1_ref[...] * lk1_ref[...], axis=-1, keepdims=True))
           - jnp.exp(jnp.sum(lq2_ref[...] * lk2_ref[...], axis=-1, keepdims=True)) + lambda_init)
    l1 = jnp.sum(l_ref[0], axis=-1, keepdims=True)
    l2 = jnp.sum(l_ref[1], axis=-1, keepdims=True)
    o = acc_ref[0] / l1 - lam * (acc_ref[1] / l2)
    o = _rms_rows(o, nw_ref[...]) * (1.0 - lambda_init)
    o_ref[...] = o.astype(o_ref.dtype)


def _attn_pos_columns(tq):
    slopes2 = np.asarray([np.float32(s * LOG2E) for s in ALIBI_SLOPES], np.float32)
    rest = np.arange(tq, dtype=np.float32)[None, :] * slopes2[:, None]
    pos = np.zeros((DIFF_HEADS, tq, LANES), BF16)
    one = np.zeros((tq, LANES), BF16)
    for col in range(3):
        part = rest.astype(BF16)
        pos[:, :, col] = part
        one[:, col] = 1.0
        rest = rest - part.astype(np.float32)
    return jnp.asarray(pos), jnp.asarray(one)


ATTN_UNDERFLOW_LOG2 = 200.0


def _attn_first_tiles(norms2, bsz, seq, tq):
    nq = seq // tq
    assert norms2.shape[0] == bsz * nq
    n = jnp.sqrt(norms2[..., 0]).reshape(bsz, nq, 2, DIFF_HEADS, 2)
    n = jnp.max(n, axis=-1)
    qn = n[:, :, 0]
    kn = jnp.max(n[:, :, 1], axis=1)
    slope_tile = jnp.asarray([np.float32(s * LOG2E) for s in ALIBI_SLOPES], F32) * tq
    reach = 1.0 + (2.0 * 1.01 * qn * kn[:, None, :] + ATTN_UNDERFLOW_LOG2) / slope_tile
    reach = jnp.where(jnp.isfinite(reach), jnp.minimum(reach, float(nq)), float(nq))
    keep = jnp.floor(reach).astype(jnp.int32)
    first = jnp.maximum(jnp.arange(nq, dtype=jnp.int32)[None, :, None] - keep, 0)
    return jnp.transpose(first, (0, 2, 1)).reshape(-1)


def _attn(proj, norms2, lq1, lk1, lq2, lk2, nw, bsz, seq, tq, lambda_init):
    nq = seq // tq
    w2 = 2 * DIFF_DK
    kern = functools.partial(_attn_kernel, tq=tq, nq=nq, lambda_init=lambda_init)
    const2 = lambda b, h, i: (0, 0)
    pos, one = _attn_pos_columns(tq)
    first = _attn_first_tiles(norms2, bsz, seq, tq)
    stat = pltpu.VMEM((2, tq, LANES), F32)
    return pl.pallas_call(
        kern,
        grid=(bsz, DIFF_HEADS, nq),
        in_specs=[
            pl.BlockSpec(memory_space=pltpu.SMEM),
            pl.BlockSpec((tq, w2), lambda b, h, i: (b * nq + i, COL_QD // w2 + h)),
            pl.BlockSpec((seq, w2), lambda b, h, i: (b, COL_KD // w2 + h),
                         pipeline_mode=pl.Buffered(1)),
            pl.BlockSpec((seq, DIFF_DV), lambda b, h, i: (b, COL_VD // DIFF_DV + h),
                         pipeline_mode=pl.Buffered(1)),
            pl.BlockSpec((1, tq, LANES), lambda b, h, i: (h, 0, 0)),
            pl.BlockSpec((tq, LANES), const2),
            pl.BlockSpec((1, DIFF_DK), const2),
            pl.BlockSpec((1, DIFF_DK), const2),
            pl.BlockSpec((1, DIFF_DK), const2),
            pl.BlockSpec((1, DIFF_DK), const2),
            pl.BlockSpec((1, DIFF_DV), const2),
        ],
        out_specs=pl.BlockSpec((tq, DIFF_DV), lambda b, h, i: (b * nq + i, h)),
        out_shape=jax.ShapeDtypeStruct((bsz * seq, DIFF_HEADS * DIFF_DV), BF16),
        scratch_shapes=[
            pltpu.VMEM((2, tq, 2 * DIFF_DK), BF16),
            pltpu.VMEM((2, tq, tq), F32),
            pltpu.VMEM((2, tq, tq), BF16),
            pltpu.VMEM((2, tq, DIFF_DV), F32),
            stat, stat, stat,
        ],
        compiler_params=_cparams(3),
        name="diffattn",
    )(first, proj, proj, proj, pos, one, lq1, lk1, lq2, lk2, nw)


def _merge_kernel(gy_ref, ob_ref, oc_ref, g0_ref, g1_ref, g2_ref, wgv_ref, wgg_ref, wb_ref, wc_ref,
                  wo_ref, h_ref, o_ref, acc_ref):
    j = pl.program_id(1)

    @pl.when(j == 0)
    def _():
        acc_ref[...] = jnp.zeros_like(acc_ref)

    gy = gy_ref[...]
    val = jnp.dot(gy, wgv_ref[...], preferred_element_type=F32)
    gate = jnp.dot(gy, wgg_ref[...], preferred_element_type=F32)
    ya = val * jax.nn.sigmoid(gate)
    yb = jnp.dot(ob_ref[...], wb_ref[...], preferred_element_type=F32)
    yc = jnp.dot(oc_ref[...], wc_ref[...], preferred_element_type=F32)
    merged = (jax.nn.sigmoid(g0_ref[...].astype(F32)) * ya
              + jax.nn.sigmoid(g1_ref[...].astype(F32)) * yb
              + jax.nn.sigmoid(g2_ref[...].astype(F32)) * yc)
    acc_ref[...] += jnp.dot(merged.astype(BF16), wo_ref[...], preferred_element_type=F32)

    @pl.when(j == pl.num_programs(1) - 1)
    def _():
        o_ref[...] = h_ref[...] + acc_ref[...]


def _merge(gy, ob, oc, proj, w_glu, wbr_g, wbr_d, w_out, h, tm, tn):
    t = h.shape[0]
    nj = D_MODEL // tn
    gt0 = COL_GT // tn
    left = lambda i, j: (i, 0)
    return pl.pallas_call(
        _merge_kernel,
        grid=(t // tm, nj),
        in_specs=[
            pl.BlockSpec((tm, D_SSM), left),
            pl.BlockSpec((tm, GLA_HEADS * GLA_DV), left),
            pl.BlockSpec((tm, DIFF_HEADS * DIFF_DV), left),
            pl.BlockSpec((tm, tn), lambda i, j: (i, gt0 + j)),
            pl.BlockSpec((tm, tn), lambda i, j: (i, gt0 + nj + j)),
            pl.BlockSpec((tm, tn), lambda i, j: (i, gt0 + 2 * nj + j)),
            pl.BlockSpec((D_SSM, tn), lambda i, j: (0, j)),
            pl.BlockSpec((D_SSM, tn), lambda i, j: (0, nj + j)),
            pl.BlockSpec((GLA_HEADS * GLA_DV, tn), lambda i, j: (0, j)),
            pl.BlockSpec((DIFF_HEADS * DIFF_DV, tn), lambda i, j: (0, j)),
            pl.BlockSpec((tn, D_MODEL), lambda i, j: (j, 0)),
            pl.BlockSpec((tm, D_MODEL), left),
        ],
        out_specs=pl.BlockSpec((tm, D_MODEL), left),
        out_shape=jax.ShapeDtypeStruct((t, D_MODEL), F32),
        scratch_shapes=[pltpu.VMEM((tm, D_MODEL), F32)],
        compiler_params=_cparams(2),
        name="merge",
    )(gy, ob, oc, proj, proj, proj, w_glu, w_glu, wbr_g, wbr_d, w_out, h)


def _ffn_kernel(h_ref, halo_ref, nw_ref, wv_ref, wg_ref, cwv_ref, cwg_ref, cbv_ref, cbg_ref, wd_ref,
                fnw_ref, o_ref, hn_ref, *, tm, tiles_per_seq, final_norm):
    i = pl.program_id(0)
    j = pl.program_id(1)

    @pl.when(j == 0)
    def _():
        keep = (i % tiles_per_seq != 0).astype(F32)
        hn_ref[0:HALO, :] = (_rms_rows(halo_ref[...], nw_ref[...]) * keep).astype(BF16)
        hn_ref[HALO:, :] = _rms_rows(h_ref[...], nw_ref[...]).astype(BF16)
        o_ref[...] = h_ref[...]

    hn = hn_ref[...]

    def conv(u, cw_ref, cb_ref, cols):
        acc = cb_ref[:, cols]
        for tap in range(CONV_W):
            lo = HALO - (CONV_W - 1) + tap
            acc = acc + u[lo:lo + tm, :] * cw_ref[tap:tap + 1, cols]
        return acc

    subs = [slice(s * FFN_SUBTILE, (s + 1) * FFN_SUBTILE) for s in range(FFN_TILE // FFN_SUBTILE)]
    ups = [(jnp.dot(hn, wv_ref[:, cols], preferred_element_type=F32),
            jnp.dot(hn, wg_ref[:, cols], preferred_element_type=F32)) for cols in subs]
    down = None
    for cols, (uv, ug) in zip(subs, ups):
        val = conv(uv, cwv_ref, cbv_ref, cols)
        gate = conv(ug, cwg_ref, cbg_ref, cols)
        act = (gate * jax.nn.sigmoid(gate) * val).astype(BF16)
        part = jnp.dot(act, wd_ref[cols, :], preferred_element_type=F32)
        down = part if down is None else down + part
    o_ref[...] += down

    if final_norm:
        @pl.when(j == pl.num_programs(1) - 1)
        def _():
            o_ref[...] = _rms_rows(o_ref[...], fnw_ref[...])


def _ffn(h, nw, wv, wg, cwv, cwg, cbv, cbg, wd, fnw, seq, tm, final_norm):
    t = h.shape[0]
    tf = FFN_TILE
    kern = functools.partial(_ffn_kernel, tm=tm, tiles_per_seq=seq // tm, final_norm=final_norm)
    halo_blocks = tm // HALO
    const2 = lambda i, j: (0, 0)
    col = lambda i, j: (0, j)
    return pl.pallas_call(
        kern,
        grid=(t // tm, D_FF_PAD // tf),
        in_specs=[
            pl.BlockSpec((tm, D_MODEL), lambda i, j: (i, 0), pipeline_mode=pl.Buffered(1)),
            pl.BlockSpec((HALO, D_MODEL), lambda i, j: (jnp.maximum(i * halo_blocks - 1, 0), 0)),
            pl.BlockSpec((1, D_MODEL), const2),
            pl.BlockSpec((D_MODEL, tf), col),
            pl.BlockSpec((D_MODEL, tf), col),
            pl.BlockSpec((CONV_W, tf), col),
            pl.BlockSpec((CONV_W, tf), col),
            pl.BlockSpec((1, tf), col),
            pl.BlockSpec((1, tf), col),
            pl.BlockSpec((tf, D_MODEL), lambda i, j: (j, 0)),
            pl.BlockSpec((1, D_MODEL), const2),
        ],
        out_specs=pl.BlockSpec((tm, D_MODEL), lambda i, j: (i, 0)),
        out_shape=jax.ShapeDtypeStruct((t, D_MODEL), F32),
        scratch_shapes=[pltpu.VMEM((HALO + tm, D_MODEL), BF16)],
        compiler_params=_cparams(2),
        name="ffn",
    )(h, h, nw, wv, wg, cwv, cwg, cbv, cbg, wd, fnw)


def _pad_cols(a, n):
    return jnp.pad(a, ((0, 0), (0, n - a.shape[1])))


def _tile(n, pref):
    return pref if n % pref == 0 else n


def kernel(x, norm_mix_w, w_in, ssm_a_re, ssm_a_im, ssm_log_dt, ssm_b_re, ssm_b_im, ssm_c_re, ssm_c_im, ssm_d, ssm_w_glu, gla_w_gate, gla_b_gate, gla_norm_w, gla_w_br, diff_lam_q1, diff_lam_k1, diff_lam_q2, diff_lam_k2, diff_norm_w, diff_w_br, w_out, norm_ffn_w, ffn_w_up, ffn_conv_w, ffn_conv_b, ffn_w_down, norm_final_w):
    bsz, seq, _ = x.shape
    t = bsz * seq
    h = x.reshape(t, D_MODEL).astype(F32)
    row = lambda v: v.reshape(1, -1).astype(F32)

    for l in range(DEPTH):
        lambda_init = 0.8 - 0.6 * math.exp(-0.3 * l)
        w = w_in[l]
        w_main = jnp.concatenate([w[:, :ALR_ORIG], w[:, ALR_ORIG + GLA_GATE_RANK:]], axis=1).astype(BF16)
        w_alr = _pad_cols(w[:, ALR_ORIG:ALR_ORIG + GLA_GATE_RANK], LANES).astype(BF16)
        tq = _tile(seq, 1024)
        proj, alr, norms2 = _inproj(h, row(norm_mix_w[l]), w_main, w_alr, tq, 1024)

        bw, cw, are, aim = _s5_weights(ssm_a_re[l], ssm_a_im[l], ssm_log_dt[l], ssm_b_re[l],
                                       ssm_b_im[l], ssm_c_re[l], ssm_c_im[l])
        gy = _s5(proj, bw, cw, are, aim, row(ssm_d[l]), bsz, seq, _tile(seq, 256))

        wg = jnp.pad(gla_w_gate[l], ((0, LANES - GLA_GATE_RANK), (0, 0))).astype(BF16)
        ob = _gla(proj, alr, wg, row(gla_b_gate[l]), row(gla_norm_w[l]), bsz, seq, _tile(seq, 256))

        oc = _attn(proj, norms2, row(diff_lam_q1[l]), row(diff_lam_k1[l]), row(diff_lam_q2[l]),
                   row(diff_lam_k2[l]), row(diff_norm_w[l]), bsz, seq, tq, lambda_init)

        h = _merge(gy, ob, oc, proj, ssm_w_glu[l].astype(BF16), gla_w_br[l].astype(BF16),
                   diff_w_br[l].astype(BF16), w_out[l].astype(BF16), h, _tile(t, 512), 512)

        wup = ffn_w_up[l]
        cwf = ffn_conv_w[l].astype(F32)
        cbf = ffn_conv_b[l].reshape(1, -1).astype(F32)
        h = _ffn(h, row(norm_ffn_w[l]),
                 _pad_cols(wup[:, :D_FF], D_FF_PAD).astype(BF16),
                 _pad_cols(wup[:, D_FF:], D_FF_PAD).astype(BF16),
                 _pad_cols(cwf[:, :D_FF], D_FF_PAD), _pad_cols(cwf[:, D_FF:], D_FF_PAD),
                 _pad_cols(cbf[:, :D_FF], D_FF_PAD), _pad_cols(cbf[:, D_FF:], D_FF_PAD),
                 jnp.pad(ffn_w_down[l], ((0, D_FF_PAD - D_FF), (0, 0))).astype(BF16),
                 row(norm_final_w), seq, _tile(seq, 1024), l == DEPTH - 1)

    return h.reshape(bsz, seq, D_MODEL).astype(x.dtype)
```
